```python
import math
import jax, jax.numpy as jnp
from jax import lax
import numpy as np

D_MODEL = 2048
BATCH = 4
SEQ = 2048
DEPTH = 4
DEC_BATCH = 128
DEC_SEQ = 1
PAST_LEN = 16384
PAGE_SIZE = 128

N_MIXERS = 2
N_RET = (DEPTH + 1) // 2
N_CONV = DEPTH // 2
RET_HEADS = 8
RET_DK = D_MODEL // RET_HEADS
RET_DV = 2 * RET_DK
RET_QK = RET_HEADS * RET_DK
RET_V = RET_HEADS * RET_DV
RET_IN = 2 * RET_QK + 2 * RET_V
RET_CHUNK = 128
ROPE_BASE = 10000.0
CONV_WIDTH = 31
D_FF = 4 * D_MODEL
EPS = 1e-6

kernel_name = "retnet_conformer_conv_hybrid_step"

F32 = jnp.float32


def _rmsnorm(x, g):
    x32 = x.astype(F32)
    y = x32 * lax.rsqrt(jnp.mean(x32 * x32, axis=-1, keepdims=True) + EPS) * g.astype(F32)
    return y.astype(x.dtype)


def _rope(x, pos):
    half = x.shape[-1] // 2
    freqs = ROPE_BASE ** (-jnp.arange(half, dtype=F32) / half)
    ang = pos[:, None] * freqs[None, :]
    cos = jnp.cos(ang)[None, :, None, :]
    sin = jnp.sin(ang)[None, :, None, :]
    x1, x2 = x[..., :half], x[..., half:]
    return jnp.concatenate([x1 * cos - x2 * sin, x1 * sin + x2 * cos], axis=-1)


def _retention(q, k, v, s0, chunk):
    b, t, h, _ = q.shape
    dv = v.shape[-1]
    nc = t // chunk
    lg = jnp.log1p(-jnp.exp2(-5.0 - jnp.arange(h, dtype=F32)))
    idx = jnp.arange(chunk, dtype=F32)
    diff = idx[:, None] - idx[None, :]
    intra = jnp.where(diff[None] >= 0.0,
                      jnp.exp(jnp.maximum(diff, 0.0)[None] * lg[:, None, None]), 0.0)
    dec_q = jnp.exp((idx[:, None] + 1.0) * lg[None, :])
    dec_k = jnp.exp((chunk - 1.0 - idx)[:, None] * lg[None, :])
    dec_c = jnp.exp(chunk * lg)

    def split(a):
        return a.reshape(b, nc, chunk, h, a.shape[-1]).swapaxes(0, 1)

    def step(s, inp):
        qc, kc, vc = inp
        sc = jnp.einsum('bihd,bjhd->bhij', qc, kc) * intra[None]
        o = (jnp.einsum('bhij,bjhe->bihe', sc, vc)
             + jnp.einsum('bihd,bhde->bihe', qc, s) * dec_q[None, :, :, None])
        s = (dec_c[None, :, None, None] * s
             + jnp.einsum('bjhd,bjhe->bhde', kc * dec_k[None, :, :, None], vc))
        return s, o

    s, o = lax.scan(step, s0, (split(q), split(k), split(v)))
    return o.swapaxes(0, 1).reshape(b, t, h, dv), s


def _retention_mixer(h, pos, s0, chunk, w_in, gn_g, w_out):
    b, t, _ = h.shape
    proj = (h @ w_in).astype(F32)
    q = proj[..., :RET_QK].reshape(b, t, RET_HEADS, RET_DK)
    k = proj[..., RET_QK:2 * RET_QK].reshape(b, t, RET_HEADS, RET_DK)
    v = proj[..., 2 * RET_QK:2 * RET_QK + RET_V].reshape(b, t, RET_HEADS, RET_DV)
    g = proj[..., 2 * RET_QK + RET_V:]
    q = _rope(q, pos)
    k = _rope(k, pos) * (RET_DK ** -0.5)
    o, s = _retention(q, k, v, s0.astype(F32), chunk)
    mu = jnp.mean(o, axis=-1, keepdims=True)
    var = jnp.mean(jnp.square(o - mu), axis=-1, keepdims=True)
    o = (o - mu) * lax.rsqrt(var + EPS) * gn_g.astype(F32)
    y = (jax.nn.silu(g) * o.reshape(b, t, RET_V)).astype(h.dtype) @ w_out
    return y, s.astype(s0.dtype)


def _conv_mixer(h, buf, w_pw1, b_pw1, w_dw, b_dw, ln_g, ln_b, w_pw2, b_pw2):
    a = h @ w_pw1 + b_pw1
    u = a[..., :D_MODEL] * jax.nn.sigmoid(a[..., D_MODEL:])
    full = jnp.concatenate([buf.astype(u.dtype), u], axis=1)
    y = lax.conv_general_dilated(full, w_dw[:, None, :].astype(u.dtype), window_strides=(1,),
                                 padding='VALID', dimension_numbers=('NWC', 'WIO', 'NWC'),
                                 feature_group_count=D_MODEL) + b_dw
    y32 = y.astype(F32)
    mu = jnp.mean(y32, axis=-1, keepdims=True)
    var = jnp.mean(jnp.square(y32 - mu), axis=-1, keepdims=True)
    z = (y32 - mu) * lax.rsqrt(var + EPS) * ln_g.astype(F32) + ln_b.astype(F32)
    z = jax.nn.silu(z).astype(h.dtype)
    return z @ w_pw2 + b_pw2, full[:, -(CONV_WIDTH - 1):]


def _trunk(x, pos, chunk, ret_state, conv_state, norm_mix_g, norm_mlp_g, norm_out_g,
           ret_w_in, ret_gn_g, ret_w_out, conv_w_pw1, conv_b_pw1, conv_w_dw, conv_b_dw,
           conv_ln_g, conv_ln_b, conv_w_pw2, conv_b_pw2, mlp_w1, mlp_w2):
    new_ret = []
    new_conv = []
    for i in range(DEPTH):
        j = i // N_MIXERS
        h = _rmsnorm(x, norm_mix_g[i])
        if i % N_MIXERS == 0:
            y, s = _retention_mixer(h, pos, ret_state[j], chunk, ret_w_in[j], ret_gn_g[j], ret_w_out[j])
            new_ret.append(s)
        else:
            y, s = _conv_mixer(h, conv_state[j], conv_w_pw1[j], conv_b_pw1[j], conv_w_dw[j], conv_b_dw[j],
                               conv_ln_g[j], conv_ln_b[j], conv_w_pw2[j], conv_b_pw2[j])
            new_conv.append(s)
        x = x + y
        h = _rmsnorm(x, norm_mlp_g[i])
        x = x + jnp.square(jax.nn.relu(h @ mlp_w1[i])) @ mlp_w2[i]
    return _rmsnorm(x, norm_out_g), jnp.stack(new_ret), jnp.stack(new_conv)


def setup_inputs(seed: int = 0) -> dict:
    key = jax.random.key(seed)
    ks = jax.random.split(key, 24)
    n = jax.random.normal
    return {
        'x_prompt': n(ks[0], (BATCH, SEQ, D_MODEL), F32),
        'x_sample': n(ks[1], (DEC_BATCH, DEC_SEQ, D_MODEL), F32),
        'state_ret': 0.5 * n(ks[2], (N_RET, DEC_BATCH, RET_HEADS, RET_DK, RET_DV), F32),
        'state_conv': 0.5 * n(ks[3], (N_CONV, DEC_BATCH, CONV_WIDTH - 1, D_MODEL), F32),
        'norm_mix_g': 1.0 + 0.02 * n(ks[4], (DEPTH, D_MODEL), F32),
        'norm_mlp_g': 1.0 + 0.02 * n(ks[5], (DEPTH, D_MODEL), F32),
        'norm_out_g': 1.0 + 0.02 * n(ks[6], (D_MODEL,), F32),
        'ret_w_in': n(ks[7], (N_RET, D_MODEL, RET_IN), F32) * D_MODEL ** -0.5,
        'ret_gn_g': 1.0 + 0.02 * n(ks[8], (N_RET, RET_HEADS, RET_DV), F32),
        'ret_w_out': n(ks[9], (N_RET, RET_V, D_MODEL), F32) * RET_V ** -0.5,
        'conv_w_pw1': n(ks[10], (N_CONV, D_MODEL, 2 * D_MODEL), F32) * D_MODEL ** -0.5,
        'conv_b_pw1': 0.02 * n(ks[11], (N_CONV, 2 * D_MODEL), F32),
        'conv_w_dw': n(ks[12], (N_CONV, CONV_WIDTH, D_MODEL), F32) * CONV_WIDTH ** -0.5,
        'conv_b_dw': 0.02 * n(ks[13], (N_CONV, D_MODEL), F32),
        'conv_ln_g': 1.0 + 0.02 * n(ks[14], (N_CONV, D_MODEL), F32),
        'conv_ln_b': 0.02 * n(ks[15], (N_CONV, D_MODEL), F32),
        'conv_w_pw2': n(ks[16], (N_CONV, D_MODEL, D_MODEL), F32) * D_MODEL ** -0.5,
        'conv_b_pw2': 0.02 * n(ks[17], (N_CONV, D_MODEL), F32),
        'mlp_w1': n(ks[18], (DEPTH, D_MODEL, D_FF), F32) * D_MODEL ** -0.5,
        'mlp_w2': n(ks[19], (DEPTH, D_FF, D_MODEL), F32) * D_FF ** -0.5,
    }


def reference(x_prompt, x_sample, state_ret, state_conv, norm_mix_g, norm_mlp_g, norm_out_g,
              ret_w_in, ret_gn_g, ret_w_out, conv_w_pw1, conv_b_pw1, conv_w_dw, conv_b_dw,
              conv_ln_g, conv_ln_b, conv_w_pw2, conv_b_pw2, mlp_w1, mlp_w2):
    weights = (norm_mix_g, norm_mlp_g, norm_out_g, ret_w_in, ret_gn_g, ret_w_out,
               conv_w_pw1, conv_b_pw1, conv_w_dw, conv_b_dw, conv_ln_g, conv_ln_b,
               conv_w_pw2, conv_b_pw2, mlp_w1, mlp_w2)
    pos_p = jnp.arange(SEQ, dtype=F32)
    ret0_p = jnp.zeros((N_RET, BATCH, RET_HEADS, RET_DK, RET_DV), state_ret.dtype)
    conv0_p = jnp.zeros((N_CONV, BATCH, CONV_WIDTH - 1, D_MODEL), x_prompt.dtype)
    chunk_p = math.gcd(SEQ, RET_CHUNK)
    y_prompt, ret_p, conv_p = _trunk(x_prompt, pos_p, chunk_p, ret0_p, conv0_p, *weights)
    pos_s = jnp.arange(DEC_SEQ, dtype=F32) + float(PAST_LEN)
    y_sample, ret_s, conv_s = _trunk(x_sample, pos_s, DEC_SEQ, state_ret, state_conv, *weights)
    return (y_prompt, y_sample, ret_p, ret_s, conv_p, conv_s)
```

```python
import functools
import math

import jax
import jax.numpy as jnp
from jax import lax
from jax.experimental import pallas as pl
from jax.experimental.pallas import tpu as pltpu

F32 = jnp.float32
BF16 = jnp.bfloat16

EPS = 1e-6
ROPE_BASE = 10000.0
PAST_LEN = 16384
RET_CHUNK = 128
RET_HEADS = 8

V7X_VMEM_LIMIT_BYTES = 56 * 1024 * 1024
LANES = 128

TM = 640
WEIGHT_CAST_ROWS = 256


def _params(n_axes):
    return pltpu.CompilerParams(dimension_semantics=("arbitrary",) * n_axes,
                                vmem_limit_bytes=V7X_VMEM_LIMIT_BYTES)


def _rmsnorm_kernel(x_ref, g_ref, o_ref):
    x = x_ref[...]
    ms = jnp.mean(x * x, axis=-1, keepdims=True)
    o_ref[...] = (x * lax.rsqrt(ms + EPS) * g_ref[...]).astype(o_ref.dtype)


def _rmsnorm(x, g3, layer, *, rows, row_block, first_block, out_dtype, name):
    d = x.shape[1]
    return pl.pallas_call(
        _rmsnorm_kernel,
        out_shape=jax.ShapeDtypeStruct((rows, d), out_dtype),
        grid=(rows // row_block,),
        in_specs=[pl.BlockSpec((row_block, d), lambda i: (i + first_block, 0)),
                  pl.BlockSpec((None, 1, d), lambda i: (layer, 0, 0))],
        out_specs=pl.BlockSpec((row_block, d), lambda i: (i, 0)),
        compiler_params=_params(1),
        name=name,
    )(x, g3)


def _cast_weights_once(w_ref, wb_ref):
    @pl.when(pl.program_id(1) == 0)
    def _():
        def body(r, carry):
            rows = pl.ds(pl.multiple_of(r * WEIGHT_CAST_ROWS, WEIGHT_CAST_ROWS), WEIGHT_CAST_ROWS)
            wb_ref[rows, :] = w_ref[rows, :].astype(BF16)
            return carry
        lax.fori_loop(0, w_ref.shape[0] // WEIGHT_CAST_ROWS, body, 0)


def _mm_rope_kernel(x_ref, w_ref, cos_ref, sin_ref, o_ref, wb_ref, *, n_q_blocks, k_scale):
    _cast_weights_once(w_ref, wb_ref)
    acc = jnp.dot(x_ref[...], wb_ref[...], preferred_element_type=F32)
    scale = jnp.where(pl.program_id(0) >= n_q_blocks, k_scale, 1.0).astype(F32)
    cos = cos_ref[...]
    sin = sin_ref[...]
    half = cos.shape[1]
    for hh in range(acc.shape[1] // (2 * half)):
        lo = slice(2 * hh * half, (2 * hh + 1) * half)
        hi = slice((2 * hh + 1) * half, (2 * hh + 2) * half)
        x1 = acc[:, lo]
        x2 = acc[:, hi]
        o_ref[:, lo] = ((x1 * cos - x2 * sin) * scale).astype(o_ref.dtype)
        o_ref[:, hi] = ((x1 * sin + x2 * cos) * scale).astype(o_ref.dtype)


def _mm_plain_kernel(x_ref, w_ref, o_ref, wb_ref, *, relu2):
    _cast_weights_once(w_ref, wb_ref)
    acc = jnp.dot(x_ref[...], wb_ref[...], preferred_element_type=F32)
    if relu2:
        acc = jnp.square(jnp.maximum(acc, 0.0))
    o_ref[...] = acc.astype(o_ref.dtype)


def _mm_glu_kernel(x_ref, wa_ref, wg_ref, ba_ref, bg_ref, o_ref, wab_ref, wgb_ref):
    _cast_weights_once(wa_ref, wab_ref)
    _cast_weights_once(wg_ref, wgb_ref)
    x = x_ref[...]
    a = jnp.dot(x, wab_ref[...], preferred_element_type=F32) + ba_ref[...]
    g = jnp.dot(x, wgb_ref[...], preferred_element_type=F32) + bg_ref[...]
    o_ref[...] = a * jax.nn.sigmoid(g)


def _mm_residual_kernel(x_ref, w_ref, r_ref, o_ref, wb_ref):
    _cast_weights_once(w_ref, wb_ref)
    acc = jnp.dot(x_ref[...], wb_ref[...], preferred_element_type=F32)
    o_ref[...] = r_ref[...] + acc


def _mm_bias_residual_kernel(x_ref, w_ref, b_ref, r_ref, o_ref, wb_ref):
    _cast_weights_once(w_ref, wb_ref)
    acc = jnp.dot(x_ref[...], wb_ref[...], preferred_element_type=F32)
    o_ref[...] = r_ref[...] + (acc + b_ref[...])


def _x_spec(k):
    return pl.BlockSpec((TM, k), lambda j, i: (i, 0))


def _w_spec(k, tn, layer, col_block_offset=0):
    return pl.BlockSpec((None, k, tn), lambda j, i: (layer, 0, j + col_block_offset))


def _tile_spec(tn):
    return pl.BlockSpec((TM, tn), lambda j, i: (i, j))


def _mm_rope(h, w, layer, cos, sin, *, n_q, tn, name):
    m, k = h.shape
    n_qk = 2 * n_q
    half = cos.shape[1]
    kern = functools.partial(_mm_rope_kernel, n_q_blocks=n_q // tn, k_scale=(2 * half) ** -0.5)
    return pl.pallas_call(
        kern,
        out_shape=jax.ShapeDtypeStruct((m, n_qk), BF16),
        grid=(n_qk // tn, m // TM),
        in_specs=[_x_spec(k), _w_spec(k, tn, layer),
                  pl.BlockSpec((TM, half), lambda j, i: (i, 0)),
                  pl.BlockSpec((TM, half), lambda j, i: (i, 0))],
        out_specs=_tile_spec(tn),
        scratch_shapes=[pltpu.VMEM((k, tn), BF16)],
        compiler_params=_params(2),
        name=name,
    )(h, w, cos, sin)


def _mm_plain(h, w, layer, *, n, col_offset, tn, out_dtype, relu2, name):
    m, k = h.shape
    return pl.pallas_call(
        functools.partial(_mm_plain_kernel, relu2=relu2),
        out_shape=jax.ShapeDtypeStruct((m, n), out_dtype),
        grid=(n // tn, m // TM),
        in_specs=[_x_spec(k), _w_spec(k, tn, layer, col_offset // tn)],
        out_specs=_tile_spec(tn),
        scratch_shapes=[pltpu.VMEM((k, tn), BF16)],
        compiler_params=_params(2),
        name=name,
    )(h, w)


def _mm_glu(h, w, b3, layer, *, tn, name):
    m, k = h.shape
    n = w.shape[2] // 2
    gate = n // tn
    return pl.pallas_call(
        _mm_glu_kernel,
        out_shape=jax.ShapeDtypeStruct((m, n), F32),
        grid=(n // tn, m // TM),
        in_specs=[_x_spec(k), _w_spec(k, tn, layer), _w_spec(k, tn, layer, gate),
                  pl.BlockSpec((None, 1, tn), lambda j, i: (layer, 0, j)),
                  pl.BlockSpec((None, 1, tn), lambda j, i: (layer, 0, j + gate))],
        out_specs=_tile_spec(tn),
        scratch_shapes=[pltpu.VMEM((k, tn), BF16), pltpu.VMEM((k, tn), BF16)],
        compiler_params=_params(2),
        name=name,
    )(h, w, w, b3, b3)


def _mm_residual(a, w, layer, res, *, tn, name, bias3=None):
    m, k = a.shape
    n = w.shape[2]
    in_specs = [_x_spec(k), _w_spec(k, tn, layer)]
    args = [a, w]
    if bias3 is not None:
        in_specs.append(pl.BlockSpec((None, 1, tn), lambda j, i: (layer, 0, j)))
        args.append(bias3)
    in_specs.append(_tile_spec(tn))
    args.append(res)
    return pl.pallas_call(
        _mm_residual_kernel if bias3 is None else _mm_bias_residual_kernel,
        out_shape=jax.ShapeDtypeStruct((m, n), F32),
        grid=(n // tn, m // TM),
        in_specs=in_specs,
        out_specs=_tile_spec(tn),
        scratch_shapes=[pltpu.VMEM((k, tn), BF16)],
        compiler_params=_params(2),
        name=name,
    )(*args)


def _group_norm_gate(o, g, gn):
    mu = jnp.mean(o, axis=-1, keepdims=True)
    var = jnp.mean(jnp.square(o - mu), axis=-1, keepdims=True)
    on = (o - mu) * lax.rsqrt(var + EPS) * gn
    return jax.nn.silu(g) * on


def _ret_prompt_kernel(q_ref, k_ref, v_ref, g_ref, intra_ref, dq_ref, dk_ref, dc_ref, gn_ref,
                       *rest, chunks_per_step):
    o_ref, s_ref = rest[-2:]

    @pl.when(pl.program_id(2) == 0)
    def _():
        s_ref[...] = jnp.zeros_like(s_ref)

    c = intra_ref.shape[0]
    for cc in range(chunks_per_step):
        rows = slice(cc * c, (cc + 1) * c)
        q = q_ref[rows, :]
        k = k_ref[rows, :]
        v = v_ref[rows, :]
        s = s_ref[...]
        sc = lax.dot_general(q, k, (((1,), (1,)), ((), ())), preferred_element_type=F32)
        sc = (sc * intra_ref[...]).astype(BF16)
        o = (jnp.dot(sc, v, preferred_element_type=F32)
             + jnp.dot(q, s.astype(BF16), preferred_element_type=F32) * dq_ref[...])
        kd_t = (k.astype(F32) * dk_ref[...]).T.astype(BF16)
        s_ref[...] = dc_ref[...] * s + jnp.dot(kd_t, v, preferred_element_type=F32)
        o_ref[rows, :] = _group_norm_gate(o, g_ref[rows, :], gn_ref[...]).astype(o_ref.dtype)


def _ret_prompt(qk, v, g, tabs, gn4, layer, rs_prev, *, batch, seq, n_ret, chunks_per_step, name):
    m = qk.shape[0]
    heads = RET_HEADS
    dk = qk.shape[1] // (2 * heads)
    dv = v.shape[1] // heads
    c = RET_CHUNK
    rows = c * chunks_per_step
    steps = seq // rows

    def tok(b, h, t):
        return (b * steps + t, h)

    in_specs = [
        pl.BlockSpec((rows, dk), tok),
        pl.BlockSpec((rows, dk), lambda b, h, t: (b * steps + t, heads + h)),
        pl.BlockSpec((rows, dv), tok),
        pl.BlockSpec((rows, dv), tok),
        pl.BlockSpec((None, c, c), lambda b, h, t: (h, 0, 0)),
        pl.BlockSpec((None, c, dv), lambda b, h, t: (h, 0, 0)),
        pl.BlockSpec((None, c, dk), lambda b, h, t: (h, 0, 0)),
        pl.BlockSpec((None, 1, dv), lambda b, h, t: (h, 0, 0)),
        pl.BlockSpec((None, None, 1, dv), lambda b, h, t: (layer, h, 0, 0)),
    ]
    args = [qk, qk, v, g, tabs["intra"], tabs["dq"], tabs["dk"], tabs["dc"], gn4]
    aliases = {}
    if rs_prev is not None:
        in_specs.append(pl.BlockSpec(memory_space=pl.ANY))
        args.append(rs_prev)
        aliases = {len(args) - 1: 1}
    return pl.pallas_call(
        functools.partial(_ret_prompt_kernel, chunks_per_step=chunks_per_step),
        out_shape=[jax.ShapeDtypeStruct((m, heads * dv), BF16),
                   jax.ShapeDtypeStruct((n_ret, batch, heads, dk, dv), F32)],
        grid=(batch, heads, steps),
        in_specs=in_specs,
        out_specs=[pl.BlockSpec((rows, dv), tok),
                   pl.BlockSpec((None, None, None, dk, dv), lambda b, h, t: (layer, b, h, 0, 0))],
        input_output_aliases=aliases,
        compiler_params=_params(3),
        name=name,
    )(*args)


def _ret_sample_kernel(*refs, nb, aliased_inputs):
    c_ref, q_ref, k_ref, v_ref, g_ref, gn_ref, s_ref = refs[:7]
    o_ref, so_ref, qf, kf, vf, oacc = refs[7 + aliased_inputs:]
    h = pl.program_id(0)
    bb = pl.program_id(1)
    dec_q = c_ref[h, 0]
    dec_c = c_ref[h, 1]
    intra = c_ref[h, 2]
    dec_k = c_ref[h, 3]

    @pl.when(bb == 0)
    def _():
        qf[...] = q_ref[...].astype(F32)
        kf[...] = k_ref[...].astype(F32)
        vf[...] = v_ref[...].astype(F32)

    n_b, dk = qf.shape
    dv = vf.shape[1]

    rows = pl.ds(pl.multiple_of(bb * nb, nb), nb)
    q_rows = qf[rows, :]
    k_rows = kf[rows, :]
    v_rows = vf[rows, :]
    o_rows = []
    for bl in range(nb):
        qrow = q_rows[bl:bl + 1, :]
        krow = k_rows[bl:bl + 1, :]
        vrow = v_rows[bl:bl + 1, :]
        qcol = jnp.broadcast_to(qrow, (LANES, dk)).T
        kcol = jnp.broadcast_to(krow * dec_k, (LANES, dk)).T
        qk = jnp.sum(qrow * krow, axis=-1, keepdims=True) * intra
        o_tiles = []
        for t in range(dv // LANES):
            lanes = slice(t * LANES, (t + 1) * LANES)
            st = s_ref[bl, :, lanes]
            vt = vrow[:, lanes]
            inter = jnp.sum(qcol * st, axis=0, keepdims=True)
            o_tiles.append(qk * vt + inter * dec_q)
            so_ref[bl, :, lanes] = dec_c * st + kcol * vt
        o_rows.append(jnp.concatenate(o_tiles, axis=1))
    oacc[rows, :] = jnp.concatenate(o_rows, axis=0)

    @pl.when(bb == pl.num_programs(1) - 1)
    def _():
        o_ref[...] = _group_norm_gate(oacc[...], g_ref[...], gn_ref[...]).astype(o_ref.dtype)


def _ret_sample(consts, qk, v, g, gn4, state_ret, gated, rs_prev, layer, *, first_row, nb, name):
    heads = RET_HEADS
    n_b = state_ret.shape[1]
    dk = state_ret.shape[3]
    dv = state_ret.shape[4]
    blk = first_row // n_b

    def tok(h, bb):
        return (blk, h)

    state_spec = pl.BlockSpec((None, nb, None, dk, dv), lambda h, bb: (layer, bb, h, 0, 0))
    in_specs = [
        pl.BlockSpec(memory_space=pltpu.SMEM),
        pl.BlockSpec((n_b, dk), tok),
        pl.BlockSpec((n_b, dk), lambda h, bb: (blk, heads + h)),
        pl.BlockSpec((n_b, dv), tok),
        pl.BlockSpec((n_b, dv), tok),
        pl.BlockSpec((None, None, 1, dv), lambda h, bb: (layer, h, 0, 0)),
        state_spec,
        pl.BlockSpec(memory_space=pl.ANY),
    ]
    args = [consts, qk, qk, v, g, gn4, state_ret, gated]
    aliases = {7: 0}
    if rs_prev is not None:
        in_specs.append(pl.BlockSpec(memory_space=pl.ANY))
        args.append(rs_prev)
        aliases[8] = 1
    return pl.pallas_call(
        functools.partial(_ret_sample_kernel, nb=nb, aliased_inputs=len(aliases)),
        out_shape=[jax.ShapeDtypeStruct(gated.shape, gated.dtype),
                   jax.ShapeDtypeStruct(state_ret.shape, state_ret.dtype)],
        grid=(heads, n_b // nb),
        in_specs=in_specs,
        out_specs=[pl.BlockSpec((n_b, dv), tok), state_spec],
        scratch_shapes=[pltpu.VMEM((n_b, dk), F32), pltpu.VMEM((n_b, dk), F32),
                        pltpu.VMEM((n_b, dv), F32), pltpu.VMEM((n_b, dv), F32)],
        input_output_aliases=aliases,
        compiler_params=_params(2),
        name=name,
    )(*args)


def _layer_norm_swish(y, g, b):
    mu = jnp.mean(y, axis=-1, keepdims=True)
    var = jnp.mean(jnp.square(y - mu), axis=-1, keepdims=True)
    z = (y - mu) * lax.rsqrt(var + EPS) * g + b
    return jax.nn.silu(z)


CONV_HALO = 32
CONV_ROW_CHUNK = 32
CONV_LANE_CHUNK = 512


def _conv_prompt_kernel(u_ref, halo_ref, w_ref, bdw_ref, lng_ref, lnb_ref, z_ref, win, y):
    tt, d = u_ref.shape
    width = w_ref.shape[0]
    n_lane_chunks = d // CONV_LANE_CHUNK
    first = pl.program_id(1) == 0
    for lc in range(n_lane_chunks):
        lanes = slice(lc * CONV_LANE_CHUNK, (lc + 1) * CONV_LANE_CHUNK)
        win[lc, 0:CONV_HALO, :] = jnp.where(first, 0.0, halo_ref[:, lanes])
        win[lc, CONV_HALO:, :] = u_ref[:, lanes]

    lead = CONV_HALO - (width - 1)

    def per_lane_chunk(lc, carry):
        lane0 = pl.multiple_of(lc * CONV_LANE_CHUNK, CONV_LANE_CHUNK)
        for rc in range(tt // CONV_ROW_CHUNK):
            r0 = rc * CONV_ROW_CHUNK
            acc = jnp.zeros((CONV_ROW_CHUNK, CONV_LANE_CHUNK), F32)
            for w in range(width):
                acc = acc + (win[lc, r0 + lead + w:r0 + lead + w + CONV_ROW_CHUNK, :]
                             * w_ref[w:w + 1, pl.ds(lane0, CONV_LANE_CHUNK)])
            y[lc, r0:r0 + CONV_ROW_CHUNK, :] = acc
        return carry

    lax.fori_loop(0, n_lane_chunks, per_lane_chunk, 0)

    yy = jnp.concatenate([y[lc] for lc in range(n_lane_chunks)], axis=-1) + bdw_ref[...]
    z_ref[...] = _layer_norm_swish(yy, lng_ref[...], lnb_ref[...]).astype(z_ref.dtype)


def _conv_prompt(u, w_dw, bdw3, lng3, lnb3, layer, *, batch, seq, tt, name):
    m, d = u.shape
    width = w_dw.shape[1]
    steps = seq // tt
    halo_per_tile = tt // CONV_HALO
    vec = pl.BlockSpec((None, 1, d), lambda b, i: (layer, 0, 0))
    return pl.pallas_call(
        _conv_prompt_kernel,
        out_shape=jax.ShapeDtypeStruct((m, d), BF16),
        grid=(batch, steps),
        in_specs=[pl.BlockSpec((tt, d), lambda b, i: (b * steps + i, 0)),
                  pl.BlockSpec((CONV_HALO, d),
                               lambda b, i: (jnp.maximum((b * steps + i) * halo_per_tile - 1, 0), 0)),
                  pl.BlockSpec((None, width, d), lambda b, i: (layer, 0, 0)),
                  vec, vec, vec],
        out_specs=pl.BlockSpec((tt, d), lambda b, i: (b * steps + i, 0)),
        scratch_shapes=[pltpu.VMEM((d // CONV_LANE_CHUNK, tt + CONV_HALO, CONV_LANE_CHUNK), F32),
                        pltpu.VMEM((d // CONV_LANE_CHUNK, tt, CONV_LANE_CHUNK), F32)],
        compiler_params=_params(2),
        name=name,
    )(u, u, w_dw, bdw3, lng3, lnb3)


def _conv_sample_kernel(*refs, aliased_inputs):
    buf_ref, u_ref, w_ref, bdw_ref, lng_ref, lnb_ref = refs[:6]
    z_ref, so_ref = refs[6 + aliased_inputs:]
    hist = buf_ref.shape[1]
    u = u_ref[...]
    acc = u * w_ref[hist:hist + 1, :]
    for w in range(hist):
        acc = acc + buf_ref[:, w, :] * w_ref[w:w + 1, :]
    y = acc + bdw_ref[...]
    z_ref[...] = _layer_norm_swish(y, lng_ref[...], lnb_ref[...]).astype(z_ref.dtype)
    for w in range(hist - 1):
        so_ref[:, w, :] = buf_ref[:, w + 1, :]
    so_ref[:, hist - 1, :] = u


def _conv_sample(state_conv, u, w_dw, bdw3, lng3, lnb3, z, cs_prev, layer, *, first_row, nb, name):
    n_b, hist, d = state_conv.shape[1:]
    width = w_dw.shape[1]
    blk0 = first_row // nb
    vec = pl.BlockSpec((None, 1, d), lambda bb: (layer, 0, 0))
    state_spec = pl.BlockSpec((None, nb, hist, d), lambda bb: (layer, bb, 0, 0))
    in_specs = [state_spec,
                pl.BlockSpec((nb, d), lambda bb: (blk0 + bb, 0)),
                pl.BlockSpec((None, width, d), lambda bb: (layer, 0, 0)),
                vec, vec, vec,
                pl.BlockSpec(memory_space=pl.ANY)]
    args = [state_conv, u, w_dw, bdw3, lng3, lnb3, z]
    aliases = {6: 0}
    if cs_prev is not None:
        in_specs.append(pl.BlockSpec(memory_space=pl.ANY))
        args.append(cs_prev)
        aliases[7] = 1
    return pl.pallas_call(
        functools.partial(_conv_sample_kernel, aliased_inputs=len(aliases)),
        out_shape=[jax.ShapeDtypeStruct(z.shape, z.dtype),
                   jax.ShapeDtypeStruct(state_conv.shape, state_conv.dtype)],
        grid=(n_b // nb,),
        in_specs=in_specs,
        out_specs=[pl.BlockSpec((nb, d), lambda bb: (blk0 + bb, 0)), state_spec],
        input_output_aliases=aliases,
        compiler_params=_params(1),
        name=name,
    )(*args)


def _rope_tables(pos, dk):
    half = dk // 2
    freqs = ROPE_BASE ** (-jnp.arange(half, dtype=F32) / half)
    ang = pos[:, None] * freqs[None, :]
    return jnp.cos(ang), jnp.sin(ang)


def _decay_tables(heads, chunk):
    lg = jnp.log1p(-jnp.exp2(-5.0 - jnp.arange(heads, dtype=F32)))
    idx = jnp.arange(chunk, dtype=F32)
    diff = idx[:, None] - idx[None, :]
    intra = jnp.where(diff[None] >= 0.0,
                      jnp.exp(jnp.maximum(diff, 0.0)[None] * lg[:, None, None]), 0.0)
    dec_q = jnp.exp((idx[:, None] + 1.0) * lg[None, :])
    dec_k = jnp.exp((chunk - 1.0 - idx)[:, None] * lg[None, :])
    dec_c = jnp.exp(chunk * lg)
    return intra, dec_q, dec_k, dec_c


def kernel(x_prompt, x_sample, state_ret, state_conv, norm_mix_g, norm_mlp_g, norm_out_g,
           ret_w_in, ret_gn_g, ret_w_out, conv_w_pw1, conv_b_pw1, conv_w_dw, conv_b_dw,
           conv_ln_g, conv_ln_b, conv_w_pw2, conv_b_pw2, mlp_w1, mlp_w2):
    batch, seq, d = x_prompt.shape
    n_dec = x_sample.shape[0]
    depth = norm_mix_g.shape[0]
    n_ret, _, heads, dk, dv = state_ret.shape
    n_conv = state_conv.shape[0]
    m_p = batch * seq
    m = m_p + n_dec
    ret_qk = heads * dk
    ret_v = heads * dv
    d_ff = mlp_w1.shape[2]
    assert heads == RET_HEADS and m % TM == 0 and x_sample.shape[1] == 1

    x = jnp.concatenate([x_prompt.reshape(m_p, d), x_sample.reshape(n_dec, d)], axis=0)

    pos = jnp.concatenate([jnp.tile(jnp.arange(seq, dtype=F32), batch),
                           jnp.arange(1, dtype=F32).repeat(n_dec) + float(PAST_LEN)])
    cos, sin = _rope_tables(pos, dk)

    chunk = math.gcd(seq, RET_CHUNK)
    intra, dec_q, dec_k, dec_c = _decay_tables(heads, chunk)
    tabs = {
        "intra": intra,
        "dq": jnp.broadcast_to(dec_q.T[:, :, None], (heads, chunk, dv)),
        "dk": jnp.broadcast_to(dec_k.T[:, :, None], (heads, chunk, dk)),
        "dc": jnp.broadcast_to(dec_c[:, None, None], (heads, 1, dv)),
    }
    intra1, dec_q1, dec_k1, dec_c1 = _decay_tables(heads, 1)
    consts1 = jnp.stack([dec_q1[0], dec_c1, intra1[:, 0, 0], dec_k1[0]], axis=1)

    mix_g3 = norm_mix_g.reshape(depth, 1, d)
    mlp_g3 = norm_mlp_g.reshape(depth, 1, d)
    out_g3 = norm_out_g.reshape(1, 1, d)
    gn4 = ret_gn_g.reshape(n_ret, heads, 1, dv)
    b_pw1_3 = conv_b_pw1.reshape(n_conv, 1, 2 * d)
    b_dw3 = conv_b_dw.reshape(n_conv, 1, d)
    ln_g3 = conv_ln_g.reshape(n_conv, 1, d)
    ln_b3 = conv_ln_b.reshape(n_conv, 1, d)
    b_pw2_3 = conv_b_pw2.reshape(n_conv, 1, d)

    rs_p = rs_s = cs_s = None
    conv_tails = []
    for i in range(depth):
        j = i // 2
        h = _rmsnorm(x, mix_g3, i, rows=m, row_block=TM, first_block=0, out_dtype=BF16,
                     name=f"norm_mix{i}")
        if i % 2 == 0:
            qk = _mm_rope(h, ret_w_in, j, cos, sin, n_q=ret_qk, tn=1024, name=f"ret{j}_qk")
            v = _mm_plain(h, ret_w_in, j, n=ret_v, col_offset=2 * ret_qk, tn=1024,
                          out_dtype=BF16, relu2=False, name=f"ret{j}_v")
            g = _mm_plain(h, ret_w_in, j, n=ret_v, col_offset=2 * ret_qk + ret_v, tn=1024,
                          out_dtype=F32, relu2=False, name=f"ret{j}_g")
            gated, rs_p = _ret_prompt(qk, v, g, tabs, gn4, j, rs_p, batch=batch, seq=seq,
                                      n_ret=n_ret, chunks_per_step=4, name=f"ret{j}_prompt")
            gated, rs_s = _ret_sample(consts1, qk, v, g, gn4, state_ret, gated, rs_s, j,
                                      first_row=m_p, nb=8, name=f"ret{j}_sample")
            x = _mm_residual(gated, ret_w_out, j, x, tn=512, name=f"ret{j}_out")
        else:
            u = _mm_glu(h, conv_w_pw1, b_pw1_3, j, tn=512, name=f"conv{j}_pw1")
            z = _conv_prompt(u, conv_w_dw, b_dw3, ln_g3, ln_b3, j, batch=batch, seq=seq, tt=256,
                             name=f"conv{j}_prompt")
            z, cs_s = _conv_sample(state_conv, u, conv_w_dw, b_dw3, ln_g3, ln_b3, z, cs_s, j,
                                   first_row=m_p, nb=16, name=f"conv{j}_sample")
            hist = state_conv.shape[2]
            conv_tails.append(u[:m_p].reshape(batch, seq, d)[:, seq - hist:])
            x = _mm_residual(z, conv_w_pw2, j, x, tn=512, bias3=b_pw2_3, name=f"conv{j}_pw2")
        h = _rmsnorm(x, mlp_g3, i, rows=m, row_block=TM, first_block=0, out_dtype=BF16,
                     name=f"norm_mlp{i}")
        a = _mm_plain(h, mlp_w1, i, n=d_ff, col_offset=0, tn=1024, out_dtype=BF16, relu2=True,
                      name=f"mlp{i}_up")
        x = _mm_residual(a, mlp_w2, i, x, tn=256, name=f"mlp{i}_down")

    y_prompt = _rmsnorm(x, out_g3, 0, rows=m_p, row_block=512, first_block=0, out_dtype=F32,
                        name="norm_out_prompt")
    y_sample = _rmsnorm(x, out_g3, 0, rows=n_dec, row_block=n_dec, first_block=m_p // n_dec,
                        out_dtype=F32, name="norm_out_sample")
    return (y_prompt.reshape(batch, seq, d), y_sample.reshape(n_dec, 1, d),
            rs_p, rs_s, jnp.stack(conv_tails), cs_s)
```

```python
import functools
import math

import jax
import jax.numpy as jnp
from jax import lax
from jax.experimental import pallas as pl
from jax.experimental.pallas import tpu as pltpu

F32 = jnp.float32
BF16 = jnp.bfloat16

EPS = 1e-6
ROPE_BASE = 10000.0
PAST_LEN = 16384
RET_CHUNK = 128
RET_HEADS = 8

V7X_VMEM_LIMIT_BYTES = 56 * 1024 * 1024
LANES = 128
SUBLANES = 8

NORM_ROWS = 640
WEIGHT_CAST_ROWS = 256


def _params(n_axes):
    return pltpu.CompilerParams(dimension_semantics=("arbitrary",) * n_axes,
                                vmem_limit_bytes=V7X_VMEM_LIMIT_BYTES)


def _rms_scale(x, g):
    ms = jnp.mean(x * x, axis=-1, keepdims=True)
    return x * lax.rsqrt(ms + EPS) * g


def _rmsnorm_kernel(x_ref, g_ref, o_ref):
    o_ref[...] = _rms_scale(x_ref[...], g_ref[...]).astype(o_ref.dtype)


def _rmsnorm(x, g3, layer, *, rows, row_block, first_block, out_dtype, name):
    d = x.shape[1]
    return pl.pallas_call(
        _rmsnorm_kernel,
        out_shape=jax.ShapeDtypeStruct((rows, d), out_dtype),
        grid=(rows // row_block,),
        in_specs=[pl.BlockSpec((row_block, d), lambda i: (i + first_block, 0)),
                  pl.BlockSpec((None, 1, d), lambda i: (layer, 0, 0))],
        out_specs=pl.BlockSpec((row_block, d), lambda i: (i, 0)),
        compiler_params=_params(1),
        name=name,
    )(x, g3)


def _cast_weights_once(w_ref, wb_ref):
    @pl.when(pl.program_id(1) == 0)
    def _():
        def body(r, carry):
            rows = pl.ds(pl.multiple_of(r * WEIGHT_CAST_ROWS, WEIGHT_CAST_ROWS), WEIGHT_CAST_ROWS)
            wb_ref[rows, :] = w_ref[rows, :].astype(BF16)
            return carry
        lax.fori_loop(0, w_ref.shape[0] // WEIGHT_CAST_ROWS, body, 0)


def _mm_rope_kernel(x_ref, w_ref, cos_ref, sin_ref, o_ref, wb_ref, *, n_q_blocks, k_scale):
    _cast_weights_once(w_ref, wb_ref)
    acc = jnp.dot(x_ref[...], wb_ref[...], preferred_element_type=F32)
    scale = jnp.where(pl.program_id(0) >= n_q_blocks, k_scale, 1.0).astype(F32)
    cos = cos_ref[...]
    sin = sin_ref[...]
    half = cos.shape[1]
    for hh in range(acc.shape[1] // (2 * half)):
        lo = slice(2 * hh * half, (2 * hh + 1) * half)
        hi = slice((2 * hh + 1) * half, (2 * hh + 2) * half)
        x1 = acc[:, lo]
        x2 = acc[:, hi]
        o_ref[:, lo] = ((x1 * cos - x2 * sin) * scale).astype(o_ref.dtype)
        o_ref[:, hi] = ((x1 * sin + x2 * cos) * scale).astype(o_ref.dtype)


def _mm_plain_kernel(x_ref, w_ref, o_ref, wb_ref, *, relu2):
    _cast_weights_once(w_ref, wb_ref)
    acc = jnp.dot(x_ref[...], wb_ref[...], preferred_element_type=F32)
    if relu2:
        acc = jnp.square(jnp.maximum(acc, 0.0))
    o_ref[...] = acc.astype(o_ref.dtype)


def _mm_glu_kernel(x_ref, wa_ref, wg_ref, ba_ref, bg_ref, o_ref, wab_ref, wgb_ref):
    _cast_weights_once(wa_ref, wab_ref)
    _cast_weights_once(wg_ref, wgb_ref)
    x = x_ref[...]
    a = jnp.dot(x, wab_ref[...], preferred_element_type=F32) + ba_ref[...]
    g = jnp.dot(x, wgb_ref[...], preferred_element_type=F32) + bg_ref[...]
    o_ref[...] = a * jax.nn.sigmoid(g)


def _mm_residual_kernel(x_ref, w_ref, r_ref, o_ref, wb_ref):
    _cast_weights_once(w_ref, wb_ref)
    acc = jnp.dot(x_ref[...], wb_ref[...], preferred_element_type=F32)
    o_ref[...] = r_ref[...] + acc


def _mm_bias_residual_norm_kernel(x_ref, w_ref, b_ref, r_ref, g_ref, o_ref, h_ref, wb_ref):
    _cast_weights_once(w_ref, wb_ref)
    acc = jnp.dot(x_ref[...], wb_ref[...], preferred_element_type=F32)
    xn = r_ref[...] + (acc + b_ref[...])
    o_ref[...] = xn
    h_ref[...] = _rms_scale(xn, g_ref[...]).astype(h_ref.dtype)


def _x_spec(tm, k):
    return pl.BlockSpec((tm, k), lambda j, i: (i, 0))


def _w_spec(k, tn, layer, col_block_offset=0, single_buffer=False):
    mode = pl.Buffered(1) if single_buffer else None
    return pl.BlockSpec((None, k, tn), lambda j, i: (layer, 0, j + col_block_offset),
                        pipeline_mode=mode)


def _tile_spec(tm, tn):
    return pl.BlockSpec((tm, tn), lambda j, i: (i, j))


def _mm_rope(h, w, layer, cos, sin, *, n_q, tm, tn, name):
    m, k = h.shape
    n_qk = 2 * n_q
    half = cos.shape[1]
    kern = functools.partial(_mm_rope_kernel, n_q_blocks=n_q // tn, k_scale=(2 * half) ** -0.5)
    return pl.pallas_call(
        kern,
        out_shape=jax.ShapeDtypeStruct((m, n_qk), BF16),
        grid=(n_qk // tn, m // tm),
        in_specs=[_x_spec(tm, k), _w_spec(k, tn, layer),
                  pl.BlockSpec((tm, half), lambda j, i: (i, 0)),
                  pl.BlockSpec((tm, half), lambda j, i: (i, 0))],
        out_specs=_tile_spec(tm, tn),
        scratch_shapes=[pltpu.VMEM((k, tn), BF16)],
        compiler_params=_params(2),
        name=name,
    )(h, w, cos, sin)


def _mm_plain(h, w, layer, *, n, col_offset, tm, tn, out_dtype, relu2, name):
    m, k = h.shape
    return pl.pallas_call(
        functools.partial(_mm_plain_kernel, relu2=relu2),
        out_shape=jax.ShapeDtypeStruct((m, n), out_dtype),
        grid=(n // tn, m // tm),
        in_specs=[_x_spec(tm, k), _w_spec(k, tn, layer, col_offset // tn)],
        out_specs=_tile_spec(tm, tn),
        scratch_shapes=[pltpu.VMEM((k, tn), BF16)],
        compiler_params=_params(2),
        name=name,
    )(h, w)


def _mm_glu(h, w, b3, layer, *, tm, tn, name):
    m, k = h.shape
    n = w.shape[2] // 2
    gate = n // tn
    return pl.pallas_call(
        _mm_glu_kernel,
        out_shape=jax.ShapeDtypeStruct((m, n), F32),
        grid=(n // tn, m // tm),
        in_specs=[_x_spec(tm, k), _w_spec(k, tn, layer), _w_spec(k, tn, layer, gate),
                  pl.BlockSpec((None, 1, tn), lambda j, i: (layer, 0, j)),
                  pl.BlockSpec((None, 1, tn), lambda j, i: (layer, 0, j + gate))],
        out_specs=_tile_spec(tm, tn),
        scratch_shapes=[pltpu.VMEM((k, tn), BF16), pltpu.VMEM((k, tn), BF16)],
        compiler_params=_params(2),
        name=name,
    )(h, w, w, b3, b3)


def _mm_residual(a, w, layer, res, *, tm, tn, name):
    m, k = a.shape
    n = w.shape[2]
    return pl.pallas_call(
        _mm_residual_kernel,
        out_shape=jax.ShapeDtypeStruct((m, n), F32),
        grid=(n // tn, m // tm),
        in_specs=[_x_spec(tm, k), _w_spec(k, tn, layer, single_buffer=True), _tile_spec(tm, tn)],
        out_specs=_tile_spec(tm, tn),
        scratch_shapes=[pltpu.VMEM((k, tn), BF16)],
        compiler_params=_params(2),
        name=name,
    )(a, w, res)


def _mm_bias_residual_norm(a, w, bias3, layer, res, g3, norm_layer, *, tm, name):
    m, k = a.shape
    n = w.shape[2]
    return pl.pallas_call(
        _mm_bias_residual_norm_kernel,
        out_shape=[jax.ShapeDtypeStruct((m, n), F32), jax.ShapeDtypeStruct((m, n), BF16)],
        grid=(1, m // tm),
        in_specs=[_x_spec(tm, k), _w_spec(k, n, layer, single_buffer=True),
                  pl.BlockSpec((None, 1, n), lambda j, i: (layer, 0, 0)),
                  _tile_spec(tm, n),
                  pl.BlockSpec((None, 1, n), lambda j, i: (norm_layer, 0, 0))],
        out_specs=[_tile_spec(tm, n), _tile_spec(tm, n)],
        scratch_shapes=[pltpu.VMEM((k, n), BF16)],
        compiler_params=_params(2),
        name=name,
    )(a, w, bias3, res, g3)


def _group_norm_gate(o, g, gn):
    mu = jnp.mean(o, axis=-1, keepdims=True)
    var = jnp.mean(jnp.square(o - mu), axis=-1, keepdims=True)
    on = (o - mu) * lax.rsqrt(var + EPS) * gn
    return jax.nn.silu(g) * on


def _ret_prompt_kernel(q_ref, k_ref, v_ref, g_ref, intra_ref, dq_ref, dk_ref, dc_ref, gn_ref,
                       *rest, chunks_per_step):
    o_ref, s_ref = rest[-2:]

    @pl.when(pl.program_id(2) == 0)
    def _():
        s_ref[...] = jnp.zeros_like(s_ref)

    c = intra_ref.shape[0]
    for cc in range(chunks_per_step):
        rows = slice(cc * c, (cc + 1) * c)
        q = q_ref[rows, :]
        k = k_ref[rows, :]
        v = v_ref[rows, :]
        s = s_ref[...]
        sc = lax.dot_general(q, k, (((1,), (1,)), ((), ())), preferred_element_type=F32)
        sc = (sc * intra_ref[...]).astype(BF16)
        o = (jnp.dot(sc, v, preferred_element_type=F32)
             + jnp.dot(q, s.astype(BF16), preferred_element_type=F32) * dq_ref[...])
        kd_t = (k.astype(F32) * dk_ref[...]).T.astype(BF16)
        s_ref[...] = dc_ref[...] * s + jnp.dot(kd_t, v, preferred_element_type=F32)
        o_ref[rows, :] = _group_norm_gate(o, g_ref[rows, :], gn_ref[...]).astype(o_ref.dtype)


def _ret_prompt(qk, v, g, tabs, gn4, layer, rs_prev, *, batch, seq, n_ret, chunks_per_step, name):
    m = qk.shape[0]
    heads = RET_HEADS
    dk = qk.shape[1] // (2 * heads)
    dv = v.shape[1] // heads
    c = RET_CHUNK
    rows = c * chunks_per_step
    steps = seq // rows

    def tok(b, h, t):
        return (b * steps + t, h)

    in_specs = [
        pl.BlockSpec((rows, dk), tok),
        pl.BlockSpec((rows, dk), lambda b, h, t: (b * steps + t, heads + h)),
        pl.BlockSpec((rows, dv), tok),
        pl.BlockSpec((rows, dv), tok),
        pl.BlockSpec((None, c, c), lambda b, h, t: (h, 0, 0)),
        pl.BlockSpec((None, c, dv), lambda b, h, t: (h, 0, 0)),
        pl.BlockSpec((None, c, dk), lambda b, h, t: (h, 0, 0)),
        pl.BlockSpec((None, 1, dv), lambda b, h, t: (h, 0, 0)),
        pl.BlockSpec((None, None, 1, dv), lambda b, h, t: (layer, h, 0, 0)),
    ]
    args = [qk, qk, v, g, tabs["intra"], tabs["dq"], tabs["dk"], tabs["dc"], gn4]
    aliases = {}
    if rs_prev is not None:
        in_specs.append(pl.BlockSpec(memory_space=pl.ANY))
        args.append(rs_prev)
        aliases = {len(args) - 1: 1}
    return pl.pallas_call(
        functools.partial(_ret_prompt_kernel, chunks_per_step=chunks_per_step),
        out_shape=[jax.ShapeDtypeStruct((m, heads * dv), BF16),
                   jax.ShapeDtypeStruct((n_ret, batch, heads, dk, dv), F32)],
        grid=(batch, heads, steps),
        in_specs=in_specs,
        out_specs=[pl.BlockSpec((rows, dv), tok),
                   pl.BlockSpec((None, None, None, dk, dv), lambda b, h, t: (layer, b, h, 0, 0))],
        input_output_aliases=aliases,
        compiler_params=_params(3),
        name=name,
    )(*args)


def _ret_sample_kernel(*refs, nb, aliased_inputs):
    c_ref, q_ref, k_ref, v_ref, g_ref, gn_ref, s_ref = refs[:7]
    o_ref, so_ref, qf, kf, vf, oacc = refs[7 + aliased_inputs:]
    h = pl.program_id(0)
    bb = pl.program_id(1)
    dec_q = c_ref[h, 0]
    dec_c = c_ref[h, 1]
    intra = c_ref[h, 2]
    dec_k = c_ref[h, 3]

    @pl.when(bb == 0)
    def _():
        qf[...] = q_ref[...].astype(F32)
        kf[...] = k_ref[...].astype(F32)
        vf[...] = v_ref[...].astype(F32)

    dk = qf.shape[1]
    dv = vf.shape[1]

    rows = pl.ds(pl.multiple_of(bb * nb, nb), nb)
    q_rows = qf[rows, :]
    k_rows = kf[rows, :]
    v_rows = vf[rows, :]
    o_rows = []
    for bl in range(nb):
        qrow = q_rows[bl:bl + 1, :]
        krow = k_rows[bl:bl + 1, :]
        vrow = v_rows[bl:bl + 1, :]
        qcol = jnp.broadcast_to(qrow, (LANES, dk)).T
        kcol = jnp.broadcast_to(krow * dec_k, (LANES, dk)).T
        qk = jnp.sum(qrow * krow, axis=-1, keepdims=True) * intra
        o_tiles = []
        for t in range(dv // LANES):
            lanes = slice(t * LANES, (t + 1) * LANES)
            st = s_ref[bl, :, lanes]
            vt = vrow[:, lanes]
            inter = jnp.sum(qcol * st, axis=0, keepdims=True)
            o_tiles.append(qk * vt + inter * dec_q)
            so_ref[bl, :, lanes] = dec_c * st + kcol * vt
        o_rows.append(jnp.concatenate(o_tiles, axis=1))
    oacc[rows, :] = jnp.concatenate(o_rows, axis=0)

    @pl.when(bb == pl.num_programs(1) - 1)
    def _():
        o_ref[...] = _group_norm_gate(oacc[...], g_ref[...], gn_ref[...]).astype(o_ref.dtype)


def _ret_sample(consts, qk, v, g, gn4, state_ret, gated, rs_prev, layer, *, first_row, name):
    heads = RET_HEADS
    nb = SUBLANES
    n_b = state_ret.shape[1]
    dk = state_ret.shape[3]
    dv = state_ret.shape[4]
    blk = first_row // n_b

    def tok(h, bb):
        return (blk, h)

    state_spec = pl.BlockSpec((None, nb, None, dk, dv), lambda h, bb: (layer, bb, h, 0, 0))
    in_specs = [
        pl.BlockSpec(memory_space=pltpu.SMEM),
        pl.BlockSpec((n_b, dk), tok),
        pl.BlockSpec((n_b, dk), lambda h, bb: (blk, heads + h)),
        pl.BlockSpec((n_b, dv), tok),
        pl.BlockSpec((n_b, dv), tok),
        pl.BlockSpec((None, None, 1, dv), lambda h, bb: (layer, h, 0, 0)),
        state_spec,
        pl.BlockSpec(memory_space=pl.ANY),
    ]
    args = [consts, qk, qk, v, g, gn4, state_ret, gated]
    aliases = {7: 0}
    if rs_prev is not None:
        in_specs.append(pl.BlockSpec(memory_space=pl.ANY))
        args.append(rs_prev)
        aliases[8] = 1
    return pl.pallas_call(
        functools.partial(_ret_sample_kernel, nb=nb, aliased_inputs=len(aliases)),
        out_shape=[jax.ShapeDtypeStruct(gated.shape, gated.dtype),
                   jax.ShapeDtypeStruct(state_ret.shape, state_ret.dtype)],
        grid=(heads, n_b // nb),
        in_specs=in_specs,
        out_specs=[pl.BlockSpec((n_b, dv), tok), state_spec],
        scratch_shapes=[pltpu.VMEM((n_b, dk), F32), pltpu.VMEM((n_b, dk), F32),
                        pltpu.VMEM((n_b, dv), F32), pltpu.VMEM((n_b, dv), F32)],
        input_output_aliases=aliases,
        compiler_params=_params(2),
        name=name,
    )(*args)


def _layer_norm_swish(y, g, b):
    mu = jnp.mean(y, axis=-1, keepdims=True)
    var = jnp.mean(jnp.square(y - mu), axis=-1, keepdims=True)
    z = (y - mu) * lax.rsqrt(var + EPS) * g + b
    return jax.nn.silu(z)


CONV_HALO = 32
CONV_ROW_CHUNK = 32
CONV_LANE_CHUNK = 512


def _conv_prompt_kernel(u_ref, halo_ref, w_ref, bdw_ref, lng_ref, lnb_ref, z_ref, tail_ref,
                        win, shifted, y):
    tt, d = u_ref.shape
    width = w_ref.shape[0]
    hist = width - 1
    lead = CONV_HALO - hist
    n_lane_chunks = d // CONV_LANE_CHUNK
    shift_rows = shifted.shape[1]
    first = pl.program_id(1) == 0
    for lc in range(n_lane_chunks):
        lanes = slice(lc * CONV_LANE_CHUNK, (lc + 1) * CONV_LANE_CHUNK)
        win[lc, 0:CONV_HALO, :] = jnp.where(first, 0.0, halo_ref[:, lanes])
        win[lc, CONV_HALO:, :] = u_ref[:, lanes]

    def per_lane_chunk(lc, carry):
        lane0 = pl.multiple_of(lc * CONV_LANE_CHUNK, CONV_LANE_CHUNK)
        for s in range(1, SUBLANES):
            shifted[s - 1, :, :] = win[lc, s:s + shift_rows, :]
        groups = CONV_ROW_CHUNK // SUBLANES
        for rc in range(tt // CONV_ROW_CHUNK):
            r0 = rc * CONV_ROW_CHUNK
            accs = [None] * groups
            for w in range(width):
                s = (lead + w) % SUBLANES
                base = r0 + lead + w - s
                wv = w_ref[w, :, pl.ds(lane0, CONV_LANE_CHUNK)]
                for rg in range(groups):
                    rows = slice(base + rg * SUBLANES, base + (rg + 1) * SUBLANES)
                    src = win[lc, rows, :] if s == 0 else shifted[s - 1, rows, :]
                    term = src * wv
                    accs[rg] = term if accs[rg] is None else accs[rg] + term
            for rg in range(groups):
                y[lc, r0 + rg * SUBLANES:r0 + (rg + 1) * SUBLANES, :] = accs[rg]
        return carry

    lax.fori_loop(0, n_lane_chunks, per_lane_chunk, 0)

    yy = jnp.concatenate([y[lc] for lc in range(n_lane_chunks)], axis=-1) + bdw_ref[...]
    z_ref[...] = _layer_norm_swish(yy, lng_ref[...], lnb_ref[...]).astype(z_ref.dtype)

    @pl.when(pl.program_id(1) == pl.num_programs(1) - 1)
    def _():
        tail_ref[...] = u_ref[tt - hist:tt, :]


def _conv_prompt(u, w_rep, bdw3, lng3, lnb3, layer, *, batch, seq, tt, name):
    m, d = u.shape
    width = w_rep.shape[1]
    hist = width - 1
    steps = seq // tt
    halo_per_tile = tt // CONV_HALO
    n_lane_chunks = d // CONV_LANE_CHUNK
    vec = pl.BlockSpec((None, 1, d), lambda b, i: (layer, 0, 0))
    return pl.pallas_call(
        _conv_prompt_kernel,
        out_shape=[jax.ShapeDtypeStruct((m, d), BF16),
                   jax.ShapeDtypeStruct((batch, hist, d), F32)],
        grid=(batch, steps),
        in_specs=[pl.BlockSpec((tt, d), lambda b, i: (b * steps + i, 0)),
                  pl.BlockSpec((CONV_HALO, d),
                               lambda b, i: (jnp.maximum((b * steps + i) * halo_per_tile - 1, 0), 0)),
                  pl.BlockSpec((None, width, SUBLANES, d), lambda b, i: (layer, 0, 0, 0)),
                  vec, vec, vec],
        out_specs=[pl.BlockSpec((tt, d), lambda b, i: (b * steps + i, 0)),
                   pl.BlockSpec((None, hist, d), lambda b, i: (b, 0, 0))],
        scratch_shapes=[pltpu.VMEM((n_lane_chunks, tt + CONV_HALO, CONV_LANE_CHUNK), F32),
                        pltpu.VMEM((SUBLANES - 1, tt + CONV_HALO - SUBLANES, CONV_LANE_CHUNK), F32),
                        pltpu.VMEM((n_lane_chunks, tt, CONV_LANE_CHUNK), F32)],
        compiler_params=_params(2),
        name=name,
    )(u, u, w_rep, bdw3, lng3, lnb3)


def _conv_sample_kernel(*refs, aliased_inputs):
    buf_ref, u_ref, w_ref, bdw_ref, lng_ref, lnb_ref = refs[:6]
    z_ref, so_ref = refs[6 + aliased_inputs:]
    hist = buf_ref.shape[0]
    u = u_ref[...]
    acc = u * w_ref[hist:hist + 1, :]
    for w in range(hist):
        acc = acc + buf_ref[w] * w_ref[w:w + 1, :]
    y = acc + bdw_ref[...]
    z_ref[...] = _layer_norm_swish(y, lng_ref[...], lnb_ref[...]).astype(z_ref.dtype)
    for w in range(hist - 1):
        so_ref[w] = buf_ref[w + 1]
    so_ref[hist - 1] = u


def _conv_sample(state_t, u, w_dw, bdw3, lng3, lnb3, z, cs_prev, layer, *, first_row, nb, name):
    hist, n_b, d = state_t.shape[1:]
    width = w_dw.shape[1]
    blk0 = first_row // nb
    vec = pl.BlockSpec((None, 1, d), lambda bb: (layer, 0, 0))
    state_spec = pl.BlockSpec((None, hist, nb, d), lambda bb: (layer, 0, bb, 0))
    in_specs = [state_spec,
                pl.BlockSpec((nb, d), lambda bb: (blk0 + bb, 0)),
                pl.BlockSpec((None, width, d), lambda bb: (layer, 0, 0)),
                vec, vec, vec,
                pl.BlockSpec(memory_space=pl.ANY)]
    args = [state_t, u, w_dw, bdw3, lng3, lnb3, z]
    aliases = {6: 0}
    if cs_prev is not None:
        in_specs.append(pl.BlockSpec(memory_space=pl.ANY))
        args.append(cs_prev)
        aliases[7] = 1
    return pl.pallas_call(
        functools.partial(_conv_sample_kernel, aliased_inputs=len(aliases)),
        out_shape=[jax.ShapeDtypeStruct(z.shape, z.dtype),
                   jax.ShapeDtypeStruct(state_t.shape, state_t.dtype)],
        grid=(n_b // nb,),
        in_specs=in_specs,
        out_specs=[pl.BlockSpec((nb, d), lambda bb: (blk0 + bb, 0)), state_spec],
        input_output_aliases=aliases,
        compiler_params=_params(1),
        name=name,
    )(*args)


def _rope_tables(pos, dk):
    half = dk // 2
    freqs = ROPE_BASE ** (-jnp.arange(half, dtype=F32) / half)
    ang = pos[:, None] * freqs[None, :]
    return jnp.cos(ang), jnp.sin(ang)


def _decay_tables(heads, chunk):
    lg = jnp.log1p(-jnp.exp2(-5.0 - jnp.arange(heads, dtype=F32)))
    idx = jnp.arange(chunk, dtype=F32)
    diff = idx[:, None] - idx[None, :]
    intra = jnp.where(diff[None] >= 0.0,
                      jnp.exp(jnp.maximum(diff, 0.0)[None] * lg[:, None, None]), 0.0)
    dec_q = jnp.exp((idx[:, None] + 1.0) * lg[None, :])
    dec_k = jnp.exp((chunk - 1.0 - idx)[:, None] * lg[None, :])
    dec_c = jnp.exp(chunk * lg)
    return intra, dec_q, dec_k, dec_c


def kernel(x_prompt, x_sample, state_ret, state_conv, norm_mix_g, norm_mlp_g, norm_out_g,
           ret_w_in, ret_gn_g, ret_w_out, conv_w_pw1, conv_b_pw1, conv_w_dw, conv_b_dw,
           conv_ln_g, conv_ln_b, conv_w_pw2, conv_b_pw2, mlp_w1, mlp_w2):
    batch, seq, d = x_prompt.shape
    n_dec = x_sample.shape[0]
    depth = norm_mix_g.shape[0]
    n_ret, _, heads, dk, dv = state_ret.shape
    n_conv = state_conv.shape[0]
    m_p = batch * seq
    m = m_p + n_dec
    ret_qk = heads * dk
    ret_v = heads * dv
    d_ff = mlp_w1.shape[2]
    assert heads == RET_HEADS and x_sample.shape[1] == 1

    tm_d = 1040
    tm_deep = 640
    tm_pw2 = 320
    assert m % tm_d == 0 and m % tm_deep == 0 and m % tm_pw2 == 0 and m % NORM_ROWS == 0

    x = jnp.concatenate([x_prompt.reshape(m_p, d), x_sample.reshape(n_dec, d)], axis=0)

    cos_p, sin_p = _rope_tables(jnp.arange(seq, dtype=F32), dk)
    cos_s, sin_s = _rope_tables(jnp.arange(1, dtype=F32) + float(PAST_LEN), dk)
    cos = jnp.concatenate([jnp.tile(cos_p, (batch, 1)), jnp.tile(cos_s, (n_dec, 1))], axis=0)
    sin = jnp.concatenate([jnp.tile(sin_p, (batch, 1)), jnp.tile(sin_s, (n_dec, 1))], axis=0)

    chunk = math.gcd(seq, RET_CHUNK)
    intra, dec_q, dec_k, dec_c = _decay_tables(heads, chunk)
    tabs = {
        "intra": intra,
        "dq": jnp.broadcast_to(dec_q.T[:, :, None], (heads, chunk, dv)),
        "dk": jnp.broadcast_to(dec_k.T[:, :, None], (heads, chunk, dk)),
        "dc": jnp.broadcast_to(dec_c[:, None, None], (heads, 1, dv)),
    }
    intra1, dec_q1, dec_k1, dec_c1 = _decay_tables(heads, 1)
    consts1 = jnp.stack([dec_q1[0], dec_c1, intra1[:, 0, 0], dec_k1[0]], axis=1)

    mix_g3 = norm_mix_g.reshape(depth, 1, d)
    mlp_g3 = norm_mlp_g.reshape(depth, 1, d)
    out_g3 = norm_out_g.reshape(1, 1, d)
    gn4 = ret_gn_g.reshape(n_ret, heads, 1, dv)
    b_pw1_3 = conv_b_pw1.reshape(n_conv, 1, 2 * d)
    b_dw3 = conv_b_dw.reshape(n_conv, 1, d)
    ln_g3 = conv_ln_g.reshape(n_conv, 1, d)
    ln_b3 = conv_ln_b.reshape(n_conv, 1, d)
    b_pw2_3 = conv_b_pw2.reshape(n_conv, 1, d)
    state_conv_t = jnp.transpose(state_conv, (0, 2, 1, 3))
    w_dw_rep = jnp.broadcast_to(conv_w_dw[:, :, None, :],
                                (n_conv, conv_w_dw.shape[1], SUBLANES, d))

    rs_p = rs_s = cs_t = None
    conv_tails = []
    for i in range(depth):
        j = i // 2
        h = _rmsnorm(x, mix_g3, i, rows=m, row_block=NORM_ROWS, first_block=0, out_dtype=BF16,
                     name=f"norm_mix{i}")
        if i % 2 == 0:
            qk = _mm_rope(h, ret_w_in, j, cos, sin, n_q=ret_qk, tm=tm_d, tn=1024, name=f"ret{j}_qk")
            v = _mm_plain(h, ret_w_in, j, n=ret_v, col_offset=2 * ret_qk, tm=tm_d, tn=1024,
                          out_dtype=BF16, relu2=False, name=f"ret{j}_v")
            g = _mm_plain(h, ret_w_in, j, n=ret_v, col_offset=2 * ret_qk + ret_v, tm=tm_d, tn=1024,
                          out_dtype=F32, relu2=False, name=f"ret{j}_g")
            gated, rs_p = _ret_prompt(qk, v, g, tabs, gn4, j, rs_p, batch=batch, seq=seq,
                                      n_ret=n_ret, chunks_per_step=8, name=f"ret{j}_prompt")
            gated, rs_s = _ret_sample(consts1, qk, v, g, gn4, state_ret, gated, rs_s, j,
                                      first_row=m_p, name=f"ret{j}_sample")
            x = _mm_residual(gated, ret_w_out, j, x, tm=tm_deep, tn=1024, name=f"ret{j}_out")
            h = _rmsnorm(x, mlp_g3, i, rows=m, row_block=NORM_ROWS, first_block=0, out_dtype=BF16,
                         name=f"norm_mlp{i}")
        else:
            u = _mm_glu(h, conv_w_pw1, b_pw1_3, j, tm=tm_d, tn=512, name=f"conv{j}_pw1")
            z, tail = _conv_prompt(u, w_dw_rep, b_dw3, ln_g3, ln_b3, j, batch=batch, seq=seq,
                                   tt=256, name=f"conv{j}_prompt")
            conv_tails.append(tail)
            z, cs_t = _conv_sample(state_conv_t, u, conv_w_dw, b_dw3, ln_g3, ln_b3, z, cs_t, j,
                                   first_row=m_p, nb=16, name=f"conv{j}_sample")
            x, h = _mm_bias_residual_norm(z, conv_w_pw2, b_pw2_3, j, x, mlp_g3, i, tm=tm_pw2,
                                          name=f"conv{j}_pw2")
        a = _mm_plain(h, mlp_w1, i, n=d_ff, col_offset=0, tm=tm_d, tn=1024, out_dtype=BF16,
                      relu2=True, name=f"mlp{i}_up")
        x = _mm_residual(a, mlp_w2, i, x, tm=tm_deep, tn=512, name=f"mlp{i}_down")

    y_prompt = _rmsnorm(x, out_g3, 0, rows=m_p, row_block=512, first_block=0, out_dtype=F32,
                        name="norm_out_prompt")
    y_sample = _rmsnorm(x, out_g3, 0, rows=n_dec, row_block=n_dec, first_block=m_p // n_dec,
                        out_dtype=F32, name="norm_out_sample")
    return (y_prompt.reshape(batch, seq, d), y_sample.reshape(n_dec, 1, d),
            rs_p, rs_s, jnp.stack(conv_tails), jnp.transpose(cs_t, (0, 2, 1, 3)))
```

```python
import functools
import math

import jax
import jax.numpy as jnp
from jax import lax
from jax.experimental import pallas as pl
from jax.experimental.pallas import tpu as pltpu

F32 = jnp.float32
BF16 = jnp.bfloat16

EPS = 1e-6
ROPE_BASE = 10000.0
PAST_LEN = 16384
RET_CHUNK = 128
RET_HEADS = 8

V7X_VMEM_LIMIT_BYTES = 56 * 1024 * 1024
LANES = 128
SUBLANES = 8

WEIGHT_CAST_ROWS = 256


def _params(n_axes):
    return pltpu.CompilerParams(dimension_semantics=("arbitrary",) * n_axes,
                                vmem_limit_bytes=V7X_VMEM_LIMIT_BYTES)


def _rms_scale(x, g):
    ms = jnp.mean(x * x, axis=-1, keepdims=True)
    return x * lax.rsqrt(ms + EPS) * g


def _rmsnorm_kernel(x_ref, g_ref, o_ref):
    o_ref[...] = _rms_scale(x_ref[...], g_ref[...]).astype(o_ref.dtype)


def _rmsnorm(x, g3, layer, *, rows, row_block, first_block, out_dtype, name):
    d = x.shape[1]
    return pl.pallas_call(
        _rmsnorm_kernel,
        out_shape=jax.ShapeDtypeStruct((rows, d), out_dtype),
        grid=(rows // row_block,),
        in_specs=[pl.BlockSpec((row_block, d), lambda i: (i + first_block, 0)),
                  pl.BlockSpec((None, 1, d), lambda i: (layer, 0, 0))],
        out_specs=pl.BlockSpec((row_block, d), lambda i: (i, 0)),
        compiler_params=_params(1),
        name=name,
    )(x, g3)


def _emit_gained(xn, g_ref, xg_ref, ssq_ref):
    xg_ref[...] = (xn * g_ref[...]).astype(xg_ref.dtype)
    sq = xn * xn
    part = sq[:, 0:LANES]
    for t in range(1, xn.shape[1] // LANES):
        part = part + sq[:, t * LANES:(t + 1) * LANES]
    ssq_ref[...] = part


def _row_rms_inv(ssq_ref, width):
    tot = jnp.sum(jnp.sum(ssq_ref[...], axis=0), axis=-1, keepdims=True)
    return lax.rsqrt(tot / width + EPS)


def _assemble_kernel(xp_ref, xs_ref, g_ref, x_ref, xg_ref, ssq_ref, *, n_prompt_blocks):
    xn = jnp.where(pl.program_id(0) < n_prompt_blocks, xp_ref[...], xs_ref[...])
    x_ref[...] = xn
    _emit_gained(xn, g_ref, xg_ref, ssq_ref)


def _assemble(xp, xs, g3, layer, *, name):
    m_p, d = xp.shape
    rb = xs.shape[0]
    m = m_p + rb
    n_prompt_blocks = m_p // rb
    return pl.pallas_call(
        functools.partial(_assemble_kernel, n_prompt_blocks=n_prompt_blocks),
        out_shape=[jax.ShapeDtypeStruct((m, d), F32), jax.ShapeDtypeStruct((m, d), BF16),
                   jax.ShapeDtypeStruct((1, m, LANES), F32)],
        grid=(m // rb,),
        in_specs=[pl.BlockSpec((rb, d), lambda i: (jnp.minimum(i, n_prompt_blocks - 1), 0)),
                  pl.BlockSpec((rb, d), lambda i: (0, 0)),
                  pl.BlockSpec((None, 1, d), lambda i: (layer, 0, 0))],
        out_specs=[pl.BlockSpec((rb, d), lambda i: (i, 0)), pl.BlockSpec((rb, d), lambda i: (i, 0)),
                   pl.BlockSpec((None, rb, LANES), lambda i: (0, i, 0))],
        compiler_params=_params(1),
        name=name,
    )(xp, xs, g3)


def _cast_weights_once(w_ref, wb_ref):
    @pl.when(pl.program_id(1) == 0)
    def _():
        def body(r, carry):
            rows = pl.ds(pl.multiple_of(r * WEIGHT_CAST_ROWS, WEIGHT_CAST_ROWS), WEIGHT_CAST_ROWS)
            wb_ref[rows, :] = w_ref[rows, :].astype(BF16)
            return carry
        lax.fori_loop(0, w_ref.shape[0] // WEIGHT_CAST_ROWS, body, 0)


def _mm_rope_kernel(x_ref, ssq_ref, w_ref, cos_ref, sin_ref, o_ref, wb_ref, *, n_q_blocks, k_scale):
    _cast_weights_once(w_ref, wb_ref)
    acc = jnp.dot(x_ref[...], wb_ref[...], preferred_element_type=F32)
    acc = acc * _row_rms_inv(ssq_ref, x_ref.shape[1])
    scale = jnp.where(pl.program_id(0) >= n_q_blocks, k_scale, 1.0).astype(F32)
    cos = cos_ref[...]
    sin = sin_ref[...]
    half = cos.shape[1]
    for hh in range(acc.shape[1] // (2 * half)):
        lo = slice(2 * hh * half, (2 * hh + 1) * half)
        hi = slice((2 * hh + 1) * half, (2 * hh + 2) * half)
        x1 = acc[:, lo]
        x2 = acc[:, hi]
        o_ref[:, lo] = ((x1 * cos - x2 * sin) * scale).astype(o_ref.dtype)
        o_ref[:, hi] = ((x1 * sin + x2 * cos) * scale).astype(o_ref.dtype)


def _mm_plain_kernel(x_ref, ssq_ref, w_ref, o_ref, wb_ref, *, relu2):
    _cast_weights_once(w_ref, wb_ref)
    acc = jnp.dot(x_ref[...], wb_ref[...], preferred_element_type=F32)
    acc = acc * _row_rms_inv(ssq_ref, x_ref.shape[1])
    if relu2:
        acc = jnp.square(jnp.maximum(acc, 0.0))
    o_ref[...] = acc.astype(o_ref.dtype)


def _mm_glu_kernel(x_ref, ssq_ref, wa_ref, wg_ref, ba_ref, bg_ref, o_ref, wab_ref, wgb_ref):
    _cast_weights_once(wa_ref, wab_ref)
    _cast_weights_once(wg_ref, wgb_ref)
    x = x_ref[...]
    r = _row_rms_inv(ssq_ref, x_ref.shape[1])
    a = jnp.dot(x, wab_ref[...], preferred_element_type=F32) * r + ba_ref[...]
    g = jnp.dot(x, wgb_ref[...], preferred_element_type=F32) * r + bg_ref[...]
    o_ref[...] = a * jax.nn.sigmoid(g)


def _mm_residual_kernel(*refs, has_bias, emit):
    refs = list(refs)
    x_ref, w_ref = refs[:2]
    del refs[:2]
    b_ref = refs.pop(0) if has_bias else None
    r_ref = refs.pop(0)
    g_ref = refs.pop(0) if emit else None
    o_ref = refs.pop(0)
    wb_ref = refs.pop()
    _cast_weights_once(w_ref, wb_ref)
    acc = jnp.dot(x_ref[...], wb_ref[...], preferred_element_type=F32)
    if has_bias:
        acc = acc + b_ref[...]
    xn = r_ref[...] + acc
    o_ref[...] = xn
    if emit:
        xg_ref, ssq_ref = refs
        _emit_gained(xn, g_ref, xg_ref, ssq_ref)


def _x_spec(tm, k):
    return pl.BlockSpec((tm, k), lambda j, i: (i, 0))


def _ssq_spec(ssq, tm):
    return pl.BlockSpec((ssq.shape[0], tm, LANES), lambda j, i: (0, i, 0))


def _w_spec(k, tn, layer, col_block_offset=0, single_buffer=False):
    mode = pl.Buffered(1) if single_buffer else None
    return pl.BlockSpec((None, k, tn), lambda j, i: (layer, 0, j + col_block_offset),
                        pipeline_mode=mode)


def _tile_spec(tm, tn):
    return pl.BlockSpec((tm, tn), lambda j, i: (i, j))


def _mm_rope(xg, ssq, w, layer, cos, sin, *, n_q, tm, tn, name):
    m, k = xg.shape
    n_qk = 2 * n_q
    half = cos.shape[1]
    kern = functools.partial(_mm_rope_kernel, n_q_blocks=n_q // tn, k_scale=(2 * half) ** -0.5)
    return pl.pallas_call(
        kern,
        out_shape=jax.ShapeDtypeStruct((m, n_qk), BF16),
        grid=(n_qk // tn, m // tm),
        in_specs=[_x_spec(tm, k), _ssq_spec(ssq, tm), _w_spec(k, tn, layer),
                  pl.BlockSpec((tm, half), lambda j, i: (i, 0)),
                  pl.BlockSpec((tm, half), lambda j, i: (i, 0))],
        out_specs=_tile_spec(tm, tn),
        scratch_shapes=[pltpu.VMEM((k, tn), BF16)],
        compiler_params=_params(2),
        name=name,
    )(xg, ssq, w, cos, sin)


def _mm_plain(xg, ssq, w, layer, *, n, col_offset, tm, tn, out_dtype, relu2, name):
    m, k = xg.shape
    return pl.pallas_call(
        functools.partial(_mm_plain_kernel, relu2=relu2),
        out_shape=jax.ShapeDtypeStruct((m, n), out_dtype),
        grid=(n // tn, m // tm),
        in_specs=[_x_spec(tm, k), _ssq_spec(ssq, tm), _w_spec(k, tn, layer, col_offset // tn)],
        out_specs=_tile_spec(tm, tn),
        scratch_shapes=[pltpu.VMEM((k, tn), BF16)],
        compiler_params=_params(2),
        name=name,
    )(xg, ssq, w)


def _mm_glu(xg, ssq, w, b3, layer, *, tm, tn, name):
    m, k = xg.shape
    n = w.shape[2] // 2
    gate = n // tn
    return pl.pallas_call(
        _mm_glu_kernel,
        out_shape=jax.ShapeDtypeStruct((m, n), F32),
        grid=(n // tn, m // tm),
        in_specs=[_x_spec(tm, k), _ssq_spec(ssq, tm),
                  _w_spec(k, tn, layer), _w_spec(k, tn, layer, gate),
                  pl.BlockSpec((None, 1, tn), lambda j, i: (layer, 0, j)),
                  pl.BlockSpec((None, 1, tn), lambda j, i: (layer, 0, j + gate))],
        out_specs=_tile_spec(tm, tn),
        scratch_shapes=[pltpu.VMEM((k, tn), BF16), pltpu.VMEM((k, tn), BF16)],
        compiler_params=_params(2),
        name=name,
    )(xg, ssq, w, w, b3, b3)


def _mm_residual(a, w, layer, res, *, tm, tn, name, bias3=None, next_gain=None):
    m, k = a.shape
    n = w.shape[2]
    in_specs = [_x_spec(tm, k), _w_spec(k, tn, layer, single_buffer=True)]
    args = [a, w]
    if bias3 is not None:
        in_specs.append(pl.BlockSpec((None, 1, tn), lambda j, i: (layer, 0, j)))
        args.append(bias3)
    in_specs.append(_tile_spec(tm, tn))
    args.append(res)
    out_shape = [jax.ShapeDtypeStruct((m, n), F32)]
    out_specs = [_tile_spec(tm, tn)]
    if next_gain is not None:
        g3, norm_layer = next_gain
        in_specs.append(pl.BlockSpec((None, 1, tn), lambda j, i: (norm_layer, 0, j)))
        args.append(g3)
        out_shape += [jax.ShapeDtypeStruct((m, n), BF16),
                      jax.ShapeDtypeStruct((n // tn, m, LANES), F32)]
        out_specs += [_tile_spec(tm, tn), pl.BlockSpec((None, tm, LANES), lambda j, i: (j, i, 0))]
    out = pl.pallas_call(
        functools.partial(_mm_residual_kernel, has_bias=bias3 is not None,
                          emit=next_gain is not None),
        out_shape=out_shape,
        grid=(n // tn, m // tm),
        in_specs=in_specs,
        out_specs=out_specs,
        scratch_shapes=[pltpu.VMEM((k, tn), BF16)],
        compiler_params=_params(2),
        name=name,
    )(*args)
    return out if next_gain is not None else out[0]


def _group_norm_gate(o, g, gn):
    mu = jnp.mean(o, axis=-1, keepdims=True)
    var = jnp.mean(jnp.square(o - mu), axis=-1, keepdims=True)
    on = (o - mu) * lax.rsqrt(var + EPS) * gn
    return jax.nn.silu(g) * on


def _ret_sample_step(c_ref, q_ref, k_ref, v_ref, g_ref, gn, st_ref, o_ref, so_ref, blk):
    h = pl.program_id(1)
    dec_q = c_ref[h, 0]
    dec_c = c_ref[h, 1]
    intra = c_ref[h, 2]
    dec_k = c_ref[h, 3]
    n_seq = o_ref.shape[0]
    dk = q_ref.shape[1]
    dv = v_ref.shape[1]
    rows = pl.ds(pl.multiple_of(blk * n_seq, n_seq), n_seq)
    q_rows = q_ref[rows, :].astype(F32)
    k_rows = k_ref[rows, :].astype(F32)
    v_rows = v_ref[rows, :].astype(F32)
    o_rows = []
    for bl in range(n_seq):
        qrow = q_rows[bl:bl + 1, :]
        krow = k_rows[bl:bl + 1, :]
        vrow = v_rows[bl:bl + 1, :]
        qcol = jnp.broadcast_to(qrow, (LANES, dk)).T
        kcol = jnp.broadcast_to(krow * dec_k, (LANES, dk)).T
        qk = jnp.sum(qrow * krow, axis=-1, keepdims=True) * intra
        o_tiles = []
        for t in range(dv // LANES):
            lanes = slice(t * LANES, (t + 1) * LANES)
            st = st_ref[bl, :, lanes]
            vt = vrow[:, lanes]
            inter = jnp.sum(qcol * st, axis=0, keepdims=True)
            o_tiles.append(qk * vt + inter * dec_q)
            so_ref[bl, :, lanes] = dec_c * st + kcol * vt
        o_rows.append(jnp.concatenate(o_tiles, axis=1))
    o = jnp.concatenate(o_rows, axis=0)
    o_ref[...] = _group_norm_gate(o, g_ref[rows, :], gn).astype(o_ref.dtype)


def _ret_prompt_step(q_ref, k_ref, v_ref, g_ref, intra_ref, dq_ref, dk_ref, dc_ref, gn, o_ref, s_ref,
                     chunks_per_step):
    @pl.when(pl.program_id(2) == 0)
    def _():
        s_ref[...] = jnp.zeros_like(s_ref)

    c = intra_ref.shape[0]
    s = s_ref[...]
    for cc in range(chunks_per_step):
        rows = slice(cc * c, (cc + 1) * c)
        q = q_ref[rows, :]
        k = k_ref[rows, :]
        v = v_ref[rows, :]
        sc = lax.dot_general(q, k, (((1,), (1,)), ((), ())), preferred_element_type=F32)
        sc = (sc * intra_ref[...]).astype(BF16)
        o = (jnp.dot(sc, v, preferred_element_type=F32)
             + jnp.dot(q, s.astype(BF16), preferred_element_type=F32) * dq_ref[...])
        kd_t = (k.astype(F32) * dk_ref[...]).T.astype(BF16)
        s = dc_ref[...] * s + jnp.dot(kd_t, v, preferred_element_type=F32)
        o_ref[rows, :] = _group_norm_gate(o, g_ref[rows, :], gn).astype(o_ref.dtype)
    s_ref[...] = s


def _retention_kernel(c_ref, q_ref, k_ref, v_ref, g_ref, intra_ref, dq_ref, dk_ref, dc_ref, gn_ref,
                      qs_ref, ks_ref, vs_ref, gs_ref, st_ref, *rest, chunks_per_step):
    o_ref, s_ref, os_ref, so_ref = rest[-4:]
    gn = gn_ref[...]
    blk = pl.program_id(0) * pl.num_programs(2) + pl.program_id(2)
    _ret_sample_step(c_ref, qs_ref, ks_ref, vs_ref, gs_ref, gn, st_ref, os_ref, so_ref, blk)
    _ret_prompt_step(q_ref, k_ref, v_ref, g_ref, intra_ref, dq_ref, dk_ref, dc_ref, gn, o_ref, s_ref,
                     chunks_per_step)


def _retention(consts, qk, v, g, tabs, gn4, state_ret, rs_p_prev, rs_s_prev, layer, *,
               batch, seq, chunks_per_step, name):
    m = qk.shape[0]
    heads = RET_HEADS
    n_ret, n_dec, _, dk, dv = state_ret.shape
    c = RET_CHUNK
    rows = c * chunks_per_step
    steps = seq // rows
    n_seq = n_dec // (batch * steps)
    sample_blk = (batch * seq) // n_dec
    assert n_seq * batch * steps == n_dec and n_seq % 16 == 0

    def tok(b, h, t):
        return (b * steps + t, h)

    def sample_tok(b, h, t):
        return (sample_blk, h)

    state_spec = pl.BlockSpec((None, n_seq, None, dk, dv), lambda b, h, t: (layer, b * steps + t, h, 0, 0))
    in_specs = [
        pl.BlockSpec(memory_space=pltpu.SMEM),
        pl.BlockSpec((rows, dk), tok),
        pl.BlockSpec((rows, dk), lambda b, h, t: (b * steps + t, heads + h)),
        pl.BlockSpec((rows, dv), tok),
        pl.BlockSpec((rows, dv), tok),
        pl.BlockSpec((None, c, c), lambda b, h, t: (h, 0, 0)),
        pl.BlockSpec((None, c, dv), lambda b, h, t: (h, 0, 0)),
        pl.BlockSpec((None, c, dk), lambda b, h, t: (h, 0, 0)),
        pl.BlockSpec((None, 1, dv), lambda b, h, t: (h, 0, 0)),
        pl.BlockSpec((None, None, 1, dv), lambda b, h, t: (layer, h, 0, 0)),
        pl.BlockSpec((n_dec, dk), sample_tok),
        pl.BlockSpec((n_dec, dk), lambda b, h, t: (sample_blk, heads + h)),
        pl.BlockSpec((n_dec, dv), sample_tok),
        pl.BlockSpec((n_dec, dv), sample_tok),
        state_spec,
    ]
    args = [consts, qk, qk, v, g, tabs["intra"], tabs["dq"], tabs["dk"], tabs["dc"], gn4,
            qk, qk, v, g, state_ret]
    aliases = {}
    if rs_p_prev is not None:
        in_specs += [pl.BlockSpec(memory_space=pl.ANY), pl.BlockSpec(memory_space=pl.ANY)]
        aliases = {len(args): 1, len(args) + 1: 3}
        args += [rs_p_prev, rs_s_prev]
    return pl.pallas_call(
        functools.partial(_retention_kernel, chunks_per_step=chunks_per_step),
        out_shape=[jax.ShapeDtypeStruct((m, heads * dv), BF16),
                   jax.ShapeDtypeStruct((n_ret, batch, heads, dk, dv), F32),
                   jax.ShapeDtypeStruct((n_dec, heads * dv), BF16),
                   jax.ShapeDtypeStruct(state_ret.shape, state_ret.dtype)],
        grid=(batch, heads, steps),
        in_specs=in_specs,
        out_specs=[pl.BlockSpec((rows, dv), tok),
                   pl.BlockSpec((None, None, None, dk, dv), lambda b, h, t: (layer, b, h, 0, 0)),
                   pl.BlockSpec((n_seq, dv), tok),
                   state_spec],
        input_output_aliases=aliases,
        compiler_params=_params(3),
        name=name,
    )(*args)


def _place_rows_kernel(src_ref, dst_ref, o_ref):
    del dst_ref
    o_ref[...] = src_ref[...]


def _place_rows(src, dst, first_row, *, name):
    n, w = src.shape
    return pl.pallas_call(
        _place_rows_kernel,
        out_shape=jax.ShapeDtypeStruct(dst.shape, dst.dtype),
        grid=(1,),
        in_specs=[pl.BlockSpec((n, w), lambda i: (0, 0)), pl.BlockSpec(memory_space=pl.ANY)],
        out_specs=pl.BlockSpec((n, w), lambda i: (first_row // n, 0)),
        input_output_aliases={1: 0},
        compiler_params=_params(1),
        name=name,
    )(src, dst)


def _layer_norm_swish(y, g, b):
    mu = jnp.mean(y, axis=-1, keepdims=True)
    var = jnp.mean(jnp.square(y - mu), axis=-1, keepdims=True)
    z = (y - mu) * lax.rsqrt(var + EPS) * g + b
    return jax.nn.silu(z)


CONV_HALO = 32
CONV_ROW_CHUNK = 32
CONV_LANE_CHUNK = 512


def _conv_prompt_kernel(u_ref, halo_ref, w_ref, bdw_ref, lng_ref, lnb_ref, z_ref, tail_ref,
                        win, shifted, y):
    tt, d = u_ref.shape
    width = w_ref.shape[0]
    hist = width - 1
    lead = CONV_HALO - hist
    n_lane_chunks = d // CONV_LANE_CHUNK
    shift_rows = shifted.shape[1]
    first = pl.program_id(1) == 0
    for lc in range(n_lane_chunks):
        lanes = slice(lc * CONV_LANE_CHUNK, (lc + 1) * CONV_LANE_CHUNK)
        win[lc, 0:CONV_HALO, :] = jnp.where(first, 0.0, halo_ref[:, lanes])
        win[lc, CONV_HALO:, :] = u_ref[:, lanes]

    def per_lane_chunk(lc, carry):
        lane0 = pl.multiple_of(lc * CONV_LANE_CHUNK, CONV_LANE_CHUNK)
        for s in range(1, SUBLANES):
            shifted[s - 1, :, :] = win[lc, s:s + shift_rows, :]
        groups = CONV_ROW_CHUNK // SUBLANES
        for rc in range(tt // CONV_ROW_CHUNK):
            r0 = rc * CONV_ROW_CHUNK
            accs = [None] * groups
            for w in range(width):
                s = (lead + w) % SUBLANES
                base = r0 + lead + w - s
                wv = w_ref[w, :, pl.ds(lane0, CONV_LANE_CHUNK)]
                for rg in range(groups):
                    rows = slice(base + rg * SUBLANES, base + (rg + 1) * SUBLANES)
                    src = win[lc, rows, :] if s == 0 else shifted[s - 1, rows, :]
                    term = src * wv
                    accs[rg] = term if accs[rg] is None else accs[rg] + term
            for rg in range(groups):
                y[lc, r0 + rg * SUBLANES:r0 + (rg + 1) * SUBLANES, :] = accs[rg]
        return carry

    lax.fori_loop(0, n_lane_chunks, per_lane_chunk, 0)

    yy = jnp.concatenate([y[lc] for lc in range(n_lane_chunks)], axis=-1) + bdw_ref[...]
    z_ref[...] = _layer_norm_swish(yy, lng_ref[...], lnb_ref[...]).astype(z_ref.dtype)

    @pl.when(pl.program_id(1) == pl.num_programs(1) - 1)
    def _():
        tail_ref[...] = u_ref[tt - hist:tt, :]


def _conv_prompt(u, w_rep, bdw3, lng3, lnb3, layer, *, batch, seq, tt, name):
    m, d = u.shape
    width = w_rep.shape[1]
    hist = width - 1
    steps = seq // tt
    halo_per_tile = tt // CONV_HALO
    n_lane_chunks = d // CONV_LANE_CHUNK
    vec = pl.BlockSpec((None, 1, d), lambda b, i: (layer, 0, 0))
    return pl.pallas_call(
        _conv_prompt_kernel,
        out_shape=[jax.ShapeDtypeStruct((m, d), BF16),
                   jax.ShapeDtypeStruct((batch, hist, d), F32)],
        grid=(batch, steps),
        in_specs=[pl.BlockSpec((tt, d), lambda b, i: (b * steps + i, 0)),
                  pl.BlockSpec((CONV_HALO, d),
                               lambda b, i: (jnp.maximum((b * steps + i) * halo_per_tile - 1, 0), 0)),
                  pl.BlockSpec((None, width, SUBLANES, d), lambda b, i: (layer, 0, 0, 0)),
                  vec, vec, vec],
        out_specs=[pl.BlockSpec((tt, d), lambda b, i: (b * steps + i, 0)),
                   pl.BlockSpec((None, hist, d), lambda b, i: (b, 0, 0))],
        scratch_shapes=[pltpu.VMEM((n_lane_chunks, tt + CONV_HALO, CONV_LANE_CHUNK), F32),
                        pltpu.VMEM((SUBLANES - 1, tt + CONV_HALO - SUBLANES, CONV_LANE_CHUNK), F32),
                        pltpu.VMEM((n_lane_chunks, tt, CONV_LANE_CHUNK), F32)],
        compiler_params=_params(2),
        name=name,
    )(u, u, w_rep, bdw3, lng3, lnb3)


def _conv_sample_kernel(*refs, aliased_inputs):
    buf_ref, u_ref, w_ref, bdw_ref, lng_ref, lnb_ref = refs[:6]
    z_ref, so_ref = refs[6 + aliased_inputs:]
    hist = buf_ref.shape[0]
    u = u_ref[...]
    acc = u * w_ref[hist:hist + 1, :]
    for w in range(hist):
        acc = acc + buf_ref[w] * w_ref[w:w + 1, :]
    y = acc + bdw_ref[...]
    z_ref[...] = _layer_norm_swish(y, lng_ref[...], lnb_ref[...]).astype(z_ref.dtype)
    for w in range(hist - 1):
        so_ref[w] = buf_ref[w + 1]
    so_ref[hist - 1] = u


def _conv_sample(state_t, u, w_dw, bdw3, lng3, lnb3, z, cs_prev, layer, *, first_row, nb, name):
    hist, n_b, d = state_t.shape[1:]
    width = w_dw.shape[1]
    blk0 = first_row // nb
    vec = pl.BlockSpec((None, 1, d), lambda bb: (layer, 0, 0))
    state_spec = pl.BlockSpec((None, hist, nb, d), lambda bb: (layer, 0, bb, 0))
    in_specs = [state_spec,
                pl.BlockSpec((nb, d), lambda bb: (blk0 + bb, 0)),
                pl.BlockSpec((None, width, d), lambda bb: (layer, 0, 0)),
                vec, vec, vec,
                pl.BlockSpec(memory_space=pl.ANY)]
    args = [state_t, u, w_dw, bdw3, lng3, lnb3, z]
    aliases = {6: 0}
    if cs_prev is not None:
        in_specs.append(pl.BlockSpec(memory_space=pl.ANY))
        args.append(cs_prev)
        aliases[7] = 1
    return pl.pallas_call(
        functools.partial(_conv_sample_kernel, aliased_inputs=len(aliases)),
        out_shape=[jax.ShapeDtypeStruct(z.shape, z.dtype),
                   jax.ShapeDtypeStruct(state_t.shape, state_t.dtype)],
        grid=(n_b // nb,),
        in_specs=in_specs,
        out_specs=[pl.BlockSpec((nb, d), lambda bb: (blk0 + bb, 0)), state_spec],
        input_output_aliases=aliases,
        compiler_params=_params(1),
        name=name,
    )(*args)


def _rope_tables(pos, dk):
    half = dk // 2
    freqs = ROPE_BASE ** (-jnp.arange(half, dtype=F32) / half)
    ang = pos[:, None] * freqs[None, :]
    return jnp.cos(ang), jnp.sin(ang)


def _decay_tables(heads, chunk):
    lg = jnp.log1p(-jnp.exp2(-5.0 - jnp.arange(heads, dtype=F32)))
    idx = jnp.arange(chunk, dtype=F32)
    diff = idx[:, None] - idx[None, :]
    intra = jnp.where(diff[None] >= 0.0,
                      jnp.exp(jnp.maximum(diff, 0.0)[None] * lg[:, None, None]), 0.0)
    dec_q = jnp.exp((idx[:, None] + 1.0) * lg[None, :])
    dec_k = jnp.exp((chunk - 1.0 - idx)[:, None] * lg[None, :])
    dec_c = jnp.exp(chunk * lg)
    return intra, dec_q, dec_k, dec_c


def kernel(x_prompt, x_sample, state_ret, state_conv, norm_mix_g, norm_mlp_g, norm_out_g,
           ret_w_in, ret_gn_g, ret_w_out, conv_w_pw1, conv_b_pw1, conv_w_dw, conv_b_dw,
           conv_ln_g, conv_ln_b, conv_w_pw2, conv_b_pw2, mlp_w1, mlp_w2):
    batch, seq, d = x_prompt.shape
    n_dec = x_sample.shape[0]
    depth = norm_mix_g.shape[0]
    n_ret, _, heads, dk, dv = state_ret.shape
    n_conv = state_conv.shape[0]
    m_p = batch * seq
    m = m_p + n_dec
    ret_qk = heads * dk
    ret_v = heads * dv
    d_ff = mlp_w1.shape[2]
    assert heads == RET_HEADS and x_sample.shape[1] == 1

    tm_d = 1040
    tm_deep = 640
    tm_pw2 = 320
    assert m % tm_d == 0 and m % tm_deep == 0 and m % tm_pw2 == 0 and m_p % n_dec == 0

    cos_p, sin_p = _rope_tables(jnp.arange(seq, dtype=F32), dk)
    cos_s, sin_s = _rope_tables(jnp.arange(1, dtype=F32) + float(PAST_LEN), dk)
    cos = jnp.concatenate([jnp.tile(cos_p, (batch, 1)), jnp.tile(cos_s, (n_dec, 1))], axis=0)
    sin = jnp.concatenate([jnp.tile(sin_p, (batch, 1)), jnp.tile(sin_s, (n_dec, 1))], axis=0)

    chunk = math.gcd(seq, RET_CHUNK)
    intra, dec_q, dec_k, dec_c = _decay_tables(heads, chunk)
    tabs = {
        "intra": intra,
        "dq": jnp.broadcast_to(dec_q.T[:, :, None], (heads, chunk, dv)),
        "dk": jnp.broadcast_to(dec_k.T[:, :, None], (heads, chunk, dk)),
        "dc": jnp.broadcast_to(dec_c[:, None, None], (heads, 1, dv)),
    }
    intra1, dec_q1, dec_k1, dec_c1 = _decay_tables(heads, 1)
    consts1 = jnp.stack([dec_q1[0], dec_c1, intra1[:, 0, 0], dec_k1[0]], axis=1)

    mix_g3 = norm_mix_g.reshape(depth, 1, d)
    mlp_g3 = norm_mlp_g.reshape(depth, 1, d)
    out_g3 = norm_out_g.reshape(1, 1, d)
    gn4 = ret_gn_g.reshape(n_ret, heads, 1, dv)
    b_pw1_3 = conv_b_pw1.reshape(n_conv, 1, 2 * d)
    b_dw3 = conv_b_dw.reshape(n_conv, 1, d)
    ln_g3 = conv_ln_g.reshape(n_conv, 1, d)
    ln_b3 = conv_ln_b.reshape(n_conv, 1, d)
    b_pw2_3 = conv_b_pw2.reshape(n_conv, 1, d)
    state_conv_t = jnp.transpose(state_conv, (0, 2, 1, 3))
    w_dw_rep = jnp.broadcast_to(conv_w_dw[:, :, None, :],
                                (n_conv, conv_w_dw.shape[1], SUBLANES, d))

    rs_p = rs_s = cs_t = None
    conv_tails = []
    x, xg, ssq = _assemble(x_prompt.reshape(m_p, d), x_sample.reshape(n_dec, d), mix_g3, 0,
                           name="assemble")
    for i in range(depth):
        j = i // 2
        if i % 2 == 0:
            qk = _mm_rope(xg, ssq, ret_w_in, j, cos, sin, n_q=ret_qk, tm=tm_d, tn=1024,
                          name=f"ret{j}_qk")
            v = _mm_plain(xg, ssq, ret_w_in, j, n=ret_v, col_offset=2 * ret_qk, tm=tm_d, tn=1024,
                          out_dtype=BF16, relu2=False, name=f"ret{j}_v")
            g = _mm_plain(xg, ssq, ret_w_in, j, n=ret_v, col_offset=2 * ret_qk + ret_v, tm=tm_d,
                          tn=1024, out_dtype=F32, relu2=False, name=f"ret{j}_g")
            gated, rs_p, gated_s, rs_s = _retention(consts1, qk, v, g, tabs, gn4, state_ret, rs_p, rs_s,
                                                    j, batch=batch, seq=seq, chunks_per_step=8,
                                                    name=f"ret{j}_mix")
            gated = _place_rows(gated_s, gated, m_p, name=f"ret{j}_place")
            x, xg, ssq = _mm_residual(gated, ret_w_out, j, x, tm=tm_deep, tn=1024,
                                      next_gain=(mlp_g3, i), name=f"ret{j}_out")
        else:
            u = _mm_glu(xg, ssq, conv_w_pw1, b_pw1_3, j, tm=tm_d, tn=512, name=f"conv{j}_pw1")
            z, tail = _conv_prompt(u, w_dw_rep, b_dw3, ln_g3, ln_b3, j, batch=batch, seq=seq,
                                   tt=256, name=f"conv{j}_prompt")
            conv_tails.append(tail)
            z, cs_t = _conv_sample(state_conv_t, u, conv_w_dw, b_dw3, ln_g3, ln_b3, z, cs_t, j,
                                   first_row=m_p, nb=16, name=f"conv{j}_sample")
            x, xg, ssq = _mm_residual(z, conv_w_pw2, j, x, tm=tm_pw2, tn=d, bias3=b_pw2_3,
                                      next_gain=(mlp_g3, i), name=f"conv{j}_pw2")
        a = _mm_plain(xg, ssq, mlp_w1, i, n=d_ff, col_offset=0, tm=tm_d, tn=1024, out_dtype=BF16,
                      relu2=True, name=f"mlp{i}_up")
        if i + 1 < depth:
            x, xg, ssq = _mm_residual(a, mlp_w2, i, x, tm=tm_deep, tn=512,
                                      next_gain=(mix_g3, i + 1), name=f"mlp{i}_down")
        else:
            x = _mm_residual(a, mlp_w2, i, x, tm=tm_deep, tn=512, name=f"mlp{i}_down")

    y_prompt = _rmsnorm(x, out_g3, 0, rows=m_p, row_block=512, first_block=0, out_dtype=F32,
                        name="norm_out_prompt")
    y_sample = _rmsnorm(x, out_g3, 0, rows=n_dec, row_block=n_dec, first_block=m_p // n_dec,
                        out_dtype=F32, name="norm_out_sample")
    return (y_prompt.reshape(batch, seq, d), y_sample.reshape(n_dec, 1, d),
            rs_p, rs_s, jnp.stack(conv_tails), jnp.transpose(cs_t, (0, 2, 1, 3)))
```

```python
import functools
import math

import jax
import jax.numpy as jnp
from jax import lax
from jax.experimental import pallas as pl
from jax.experimental.pallas import tpu as pltpu
from jax.experimental.pallas import tpu_sc as plsc

F32 = jnp.float32
BF16 = jnp.bfloat16

EPS = 1e-6
ROPE_BASE = 10000.0
PAST_LEN = 16384
RET_CHUNK = 128
RET_HEADS = 8

V7X_VMEM_LIMIT_BYTES = 56 * 1024 * 1024
LANES = 128
SUBLANES = 8

WEIGHT_CAST_ROWS = 256


def _params(n_axes):
    return pltpu.CompilerParams(dimension_semantics=("arbitrary",) * n_axes,
                                vmem_limit_bytes=V7X_VMEM_LIMIT_BYTES)


def _rms_scale(x, g):
    ms = jnp.mean(x * x, axis=-1, keepdims=True)
    return x * lax.rsqrt(ms + EPS) * g


def _rmsnorm_kernel(x_ref, g_ref, o_ref):
    o_ref[...] = _rms_scale(x_ref[...], g_ref[...]).astype(o_ref.dtype)


def _rmsnorm(x, g3, layer, *, rows, row_block, first_block, out_dtype, name):
    d = x.shape[1]
    return pl.pallas_call(
        _rmsnorm_kernel,
        out_shape=jax.ShapeDtypeStruct((rows, d), out_dtype),
        grid=(rows // row_block,),
        in_specs=[pl.BlockSpec((row_block, d), lambda i: (i + first_block, 0)),
                  pl.BlockSpec((None, 1, d), lambda i: (layer, 0, 0))],
        out_specs=pl.BlockSpec((row_block, d), lambda i: (i, 0)),
        compiler_params=_params(1),
        name=name,
    )(x, g3)


def _emit_gained(xn, g_ref, xg_ref, ssq_ref):
    xg_ref[...] = (xn * g_ref[...]).astype(xg_ref.dtype)
    sq = xn * xn
    part = sq[:, 0:LANES]
    for t in range(1, xn.shape[1] // LANES):
        part = part + sq[:, t * LANES:(t + 1) * LANES]
    ssq_ref[...] = part


def _row_rms_inv(ssq_ref, width):
    tot = jnp.sum(jnp.sum(ssq_ref[...], axis=0), axis=-1, keepdims=True)
    return lax.rsqrt(tot / width + EPS)


def _assemble_kernel(xp_ref, xs_ref, g_ref, x_ref, xg_ref, ssq_ref, *, n_prompt_blocks):
    xn = jnp.where(pl.program_id(0) < n_prompt_blocks, xp_ref[...], xs_ref[...])
    x_ref[...] = xn
    _emit_gained(xn, g_ref, xg_ref, ssq_ref)


def _assemble(xp, xs, g3, layer, *, name):
    m_p, d = xp.shape
    rb = xs.shape[0]
    m = m_p + rb
    n_prompt_blocks = m_p // rb
    return pl.pallas_call(
        functools.partial(_assemble_kernel, n_prompt_blocks=n_prompt_blocks),
        out_shape=[jax.ShapeDtypeStruct((m, d), F32), jax.ShapeDtypeStruct((m, d), BF16),
                   jax.ShapeDtypeStruct((1, m, LANES), F32)],
        grid=(m // rb,),
        in_specs=[pl.BlockSpec((rb, d), lambda i: (jnp.minimum(i, n_prompt_blocks - 1), 0)),
                  pl.BlockSpec((rb, d), lambda i: (0, 0)),
                  pl.BlockSpec((None, 1, d), lambda i: (layer, 0, 0))],
        out_specs=[pl.BlockSpec((rb, d), lambda i: (i, 0)), pl.BlockSpec((rb, d), lambda i: (i, 0)),
                   pl.BlockSpec((None, rb, LANES), lambda i: (0, i, 0))],
        compiler_params=_params(1),
        name=name,
    )(xp, xs, g3)


def _cast_weights_once(w_ref, wb_ref):
    @pl.when(pl.program_id(1) == 0)
    def _():
        def body(r, carry):
            rows = pl.ds(pl.multiple_of(r * WEIGHT_CAST_ROWS, WEIGHT_CAST_ROWS), WEIGHT_CAST_ROWS)
            wb_ref[rows, :] = w_ref[rows, :].astype(BF16)
            return carry
        lax.fori_loop(0, w_ref.shape[0] // WEIGHT_CAST_ROWS, body, 0)


def _mm_rope_kernel(x_ref, ssq_ref, w_ref, cos_ref, sin_ref, o_ref, wb_ref, *, n_q_blocks, k_scale):
    _cast_weights_once(w_ref, wb_ref)
    acc = jnp.dot(x_ref[...], wb_ref[...], preferred_element_type=F32)
    acc = acc * _row_rms_inv(ssq_ref, x_ref.shape[1])
    scale = jnp.where(pl.program_id(0) >= n_q_blocks, k_scale, 1.0).astype(F32)
    cos = cos_ref[...]
    sin = sin_ref[...]
    half = cos.shape[1]
    for hh in range(acc.shape[1] // (2 * half)):
        lo = slice(2 * hh * half, (2 * hh + 1) * half)
        hi = slice((2 * hh + 1) * half, (2 * hh + 2) * half)
        x1 = acc[:, lo]
        x2 = acc[:, hi]
        o_ref[:, lo] = ((x1 * cos - x2 * sin) * scale).astype(o_ref.dtype)
        o_ref[:, hi] = ((x1 * sin + x2 * cos) * scale).astype(o_ref.dtype)


def _mm_plain_kernel(x_ref, ssq_ref, w_ref, o_ref, wb_ref, *, relu2):
    _cast_weights_once(w_ref, wb_ref)
    acc = jnp.dot(x_ref[...], wb_ref[...], preferred_element_type=F32)
    acc = acc * _row_rms_inv(ssq_ref, x_ref.shape[1])
    if relu2:
        acc = jnp.square(jnp.maximum(acc, 0.0))
    o_ref[...] = acc.astype(o_ref.dtype)


def _mm_glu_kernel(x_ref, ssq_ref, wa_ref, wg_ref, ba_ref, bg_ref, o_ref, wab_ref, wgb_ref):
    _cast_weights_once(wa_ref, wab_ref)
    _cast_weights_once(wg_ref, wgb_ref)
    x = x_ref[...]
    r = _row_rms_inv(ssq_ref, x_ref.shape[1])
    a = jnp.dot(x, wab_ref[...], preferred_element_type=F32) * r + ba_ref[...]
    g = jnp.dot(x, wgb_ref[...], preferred_element_type=F32) * r + bg_ref[...]
    o_ref[...] = a * jax.nn.sigmoid(g)


def _mm_residual_kernel(*refs, has_bias, emit):
    refs = list(refs)
    x_ref, w_ref = refs[:2]
    del refs[:2]
    b_ref = refs.pop(0) if has_bias else None
    r_ref = refs.pop(0)
    g_ref = refs.pop(0) if emit else None
    o_ref = refs.pop(0)
    wb_ref = refs.pop()
    _cast_weights_once(w_ref, wb_ref)
    acc = jnp.dot(x_ref[...], wb_ref[...], preferred_element_type=F32)
    if has_bias:
        acc = acc + b_ref[...]
    xn = r_ref[...] + acc
    o_ref[...] = xn
    if emit:
        xg_ref, ssq_ref = refs
        _emit_gained(xn, g_ref, xg_ref, ssq_ref)


def _x_spec(tm, k):
    return pl.BlockSpec((tm, k), lambda j, i: (i, 0))


def _ssq_spec(ssq, tm):
    return pl.BlockSpec((ssq.shape[0], tm, LANES), lambda j, i: (0, i, 0))


def _w_spec(k, tn, layer, col_block_offset=0, single_buffer=False):
    mode = pl.Buffered(1) if single_buffer else None
    return pl.BlockSpec((None, k, tn), lambda j, i: (layer, 0, j + col_block_offset),
                        pipeline_mode=mode)


def _tile_spec(tm, tn):
    return pl.BlockSpec((tm, tn), lambda j, i: (i, j))


def _mm_rope(xg, ssq, w, layer, cos, sin, *, n_q, tm, tn, name):
    m, k = xg.shape
    n_qk = 2 * n_q
    half = cos.shape[1]
    kern = functools.partial(_mm_rope_kernel, n_q_blocks=n_q // tn, k_scale=(2 * half) ** -0.5)
    return pl.pallas_call(
        kern,
        out_shape=jax.ShapeDtypeStruct((m, n_qk), BF16),
        grid=(n_qk // tn, m // tm),
        in_specs=[_x_spec(tm, k), _ssq_spec(ssq, tm), _w_spec(k, tn, layer),
                  pl.BlockSpec((tm, half), lambda j, i: (i, 0)),
                  pl.BlockSpec((tm, half), lambda j, i: (i, 0))],
        out_specs=_tile_spec(tm, tn),
        scratch_shapes=[pltpu.VMEM((k, tn), BF16)],
        compiler_params=_params(2),
        name=name,
    )(xg, ssq, w, cos, sin)


def _mm_plain(xg, ssq, w, layer, *, n, col_offset, tm, tn, out_dtype, relu2, name):
    m, k = xg.shape
    return pl.pallas_call(
        functools.partial(_mm_plain_kernel, relu2=relu2),
        out_shape=jax.ShapeDtypeStruct((m, n), out_dtype),
        grid=(n // tn, m // tm),
        in_specs=[_x_spec(tm, k), _ssq_spec(ssq, tm), _w_spec(k, tn, layer, col_offset // tn)],
        out_specs=_tile_spec(tm, tn),
        scratch_shapes=[pltpu.VMEM((k, tn), BF16)],
        compiler_params=_params(2),
        name=name,
    )(xg, ssq, w)


def _mm_glu(xg, ssq, w, b3, layer, *, tm, tn, name):
    m, k = xg.shape
    n = w.shape[2] // 2
    gate = n // tn
    return pl.pallas_call(
        _mm_glu_kernel,
        out_shape=jax.ShapeDtypeStruct((m, n), F32),
        grid=(n // tn, m // tm),
        in_specs=[_x_spec(tm, k), _ssq_spec(ssq, tm),
                  _w_spec(k, tn, layer), _w_spec(k, tn, layer, gate),
                  pl.BlockSpec((None, 1, tn), lambda j, i: (layer, 0, j)),
                  pl.BlockSpec((None, 1, tn), lambda j, i: (layer, 0, j + gate))],
        out_specs=_tile_spec(tm, tn),
        scratch_shapes=[pltpu.VMEM((k, tn), BF16), pltpu.VMEM((k, tn), BF16)],
        compiler_params=_params(2),
        name=name,
    )(xg, ssq, w, w, b3, b3)


def _mm_residual(a, w, layer, res, *, tm, tn, name, bias3=None, next_gain=None):
    m, k = a.shape
    n = w.shape[2]
    in_specs = [_x_spec(tm, k), _w_spec(k, tn, layer, single_buffer=True)]
    args = [a, w]
    if bias3 is not None:
        in_specs.append(pl.BlockSpec((None, 1, tn), lambda j, i: (layer, 0, j)))
        args.append(bias3)
    in_specs.append(_tile_spec(tm, tn))
    args.append(res)
    out_shape = [jax.ShapeDtypeStruct((m, n), F32)]
    out_specs = [_tile_spec(tm, tn)]
    if next_gain is not None:
        g3, norm_layer = next_gain
        in_specs.append(pl.BlockSpec((None, 1, tn), lambda j, i: (norm_layer, 0, j)))
        args.append(g3)
        out_shape += [jax.ShapeDtypeStruct((m, n), BF16),
                      jax.ShapeDtypeStruct((n // tn, m, LANES), F32)]
        out_specs += [_tile_spec(tm, tn), pl.BlockSpec((None, tm, LANES), lambda j, i: (j, i, 0))]
    out = pl.pallas_call(
        functools.partial(_mm_residual_kernel, has_bias=bias3 is not None,
                          emit=next_gain is not None),
        out_shape=out_shape,
        grid=(n // tn, m // tm),
        in_specs=in_specs,
        out_specs=out_specs,
        scratch_shapes=[pltpu.VMEM((k, tn), BF16)],
        compiler_params=_params(2),
        name=name,
    )(*args)
    return out if next_gain is not None else out[0]


def _group_norm_gate(o, g, gn):
    mu = jnp.mean(o, axis=-1, keepdims=True)
    var = jnp.mean(jnp.square(o - mu), axis=-1, keepdims=True)
    on = (o - mu) * lax.rsqrt(var + EPS) * gn
    return jax.nn.silu(g) * on


def _ret_sample_step(c_ref, q_ref, k_ref, v_ref, g_ref, gn, st_ref, o_ref, so_ref, blk):
    h = pl.program_id(1)
    dec_q = c_ref[h, 0]
    dec_c = c_ref[h, 1]
    intra = c_ref[h, 2]
    dec_k = c_ref[h, 3]
    n_seq = o_ref.shape[0]
    dk = q_ref.shape[1]
    dv = v_ref.shape[1]
    rows = pl.ds(pl.multiple_of(blk * n_seq, n_seq), n_seq)
    q_rows = q_ref[rows, :].astype(F32)
    k_rows = k_ref[rows, :].astype(F32)
    v_rows = v_ref[rows, :].astype(F32)
    o_rows = []
    for bl in range(n_seq):
        qrow = q_rows[bl:bl + 1, :]
        krow = k_rows[bl:bl + 1, :]
        vrow = v_rows[bl:bl + 1, :]
        qcol = jnp.broadcast_to(qrow, (LANES, dk)).T
        if so_ref is not None:
            kcol = jnp.broadcast_to(krow * dec_k, (LANES, dk)).T
        qk = jnp.sum(qrow * krow, axis=-1, keepdims=True) * intra
        o_tiles = []
        for t in range(dv // LANES):
            lanes = slice(t * LANES, (t + 1) * LANES)
            st = st_ref[bl, :, lanes]
            vt = vrow[:, lanes]
            inter = jnp.sum(qcol * st, axis=0, keepdims=True)
            o_tiles.append(qk * vt + inter * dec_q)
            if so_ref is not None:
                so_ref[bl, :, lanes] = dec_c * st + kcol * vt
        o_rows.append(jnp.concatenate(o_tiles, axis=1))
    o = jnp.concatenate(o_rows, axis=0)
    o_ref[...] = _group_norm_gate(o, g_ref[rows, :], gn).astype(o_ref.dtype)


def _ret_prompt_step(q_ref, k_ref, v_ref, g_ref, intra_ref, dq_ref, dk_ref, dc_ref, gn, o_ref, s_ref,
                     chunks_per_step):
    @pl.when(pl.program_id(2) == 0)
    def _():
        s_ref[...] = jnp.zeros_like(s_ref)

    c = intra_ref.shape[0]
    s = s_ref[...]
    for cc in range(chunks_per_step):
        rows = slice(cc * c, (cc + 1) * c)
        q = q_ref[rows, :]
        k = k_ref[rows, :]
        v = v_ref[rows, :]
        sc = lax.dot_general(q, k, (((1,), (1,)), ((), ())), preferred_element_type=F32)
        sc = (sc * intra_ref[...]).astype(BF16)
        o = (jnp.dot(sc, v, preferred_element_type=F32)
             + jnp.dot(q, s.astype(BF16), preferred_element_type=F32) * dq_ref[...])
        kd_t = (k.astype(F32) * dk_ref[...]).T.astype(BF16)
        s = dc_ref[...] * s + jnp.dot(kd_t, v, preferred_element_type=F32)
        o_ref[rows, :] = _group_norm_gate(o, g_ref[rows, :], gn).astype(o_ref.dtype)
    s_ref[...] = s


def _retention_kernel(c_ref, q_ref, k_ref, v_ref, g_ref, intra_ref, dq_ref, dk_ref, dc_ref, gn_ref,
                      qs_ref, ks_ref, vs_ref, gs_ref, st_ref, *rest, chunks_per_step, update_state):
    n_out = 4 if update_state else 3
    o_ref, s_ref, os_ref = rest[-n_out:][:3]
    so_ref = rest[-1] if update_state else None
    gn = gn_ref[...]
    blk = pl.program_id(0) * pl.num_programs(2) + pl.program_id(2)
    _ret_sample_step(c_ref, qs_ref, ks_ref, vs_ref, gs_ref, gn, st_ref, os_ref, so_ref, blk)
    _ret_prompt_step(q_ref, k_ref, v_ref, g_ref, intra_ref, dq_ref, dk_ref, dc_ref, gn, o_ref, s_ref,
                     chunks_per_step)


def _retention(consts, qk, v, g, tabs, gn4, state_ret, rs_p_prev, rs_s_prev, layer, *,
               batch, seq, chunks_per_step, name):
    m = qk.shape[0]
    heads = RET_HEADS
    n_ret, n_dec, _, dk, dv = state_ret.shape
    c = RET_CHUNK
    rows = c * chunks_per_step
    steps = seq // rows
    n_seq = n_dec // (batch * steps)
    sample_blk = (batch * seq) // n_dec
    assert n_seq * batch * steps == n_dec and n_seq % 16 == 0

    def tok(b, h, t):
        return (b * steps + t, h)

    def sample_tok(b, h, t):
        return (sample_blk, h)

    state_spec = pl.BlockSpec((None, n_seq, None, dk, dv), lambda b, h, t: (layer, b * steps + t, h, 0, 0))
    in_specs = [
        pl.BlockSpec(memory_space=pltpu.SMEM),
        pl.BlockSpec((rows, dk), tok),
        pl.BlockSpec((rows, dk), lambda b, h, t: (b * steps + t, heads + h)),
        pl.BlockSpec((rows, dv), tok),
        pl.BlockSpec((rows, dv), tok),
        pl.BlockSpec((None, c, c), lambda b, h, t: (h, 0, 0)),
        pl.BlockSpec((None, c, dv), lambda b, h, t: (h, 0, 0)),
        pl.BlockSpec((None, c, dk), lambda b, h, t: (h, 0, 0)),
        pl.BlockSpec((None, 1, dv), lambda b, h, t: (h, 0, 0)),
        pl.BlockSpec((None, None, 1, dv), lambda b, h, t: (layer, h, 0, 0)),
        pl.BlockSpec((n_dec, dk), sample_tok),
        pl.BlockSpec((n_dec, dk), lambda b, h, t: (sample_blk, heads + h)),
        pl.BlockSpec((n_dec, dv), sample_tok),
        pl.BlockSpec((n_dec, dv), sample_tok),
        state_spec,
    ]
    args = [consts, qk, qk, v, g, tabs["intra"], tabs["dq"], tabs["dk"], tabs["dc"], gn4,
            qk, qk, v, g, state_ret]
    out_shape = [jax.ShapeDtypeStruct((m, heads * dv), BF16),
                 jax.ShapeDtypeStruct((n_ret, batch, heads, dk, dv), F32),
                 jax.ShapeDtypeStruct((n_dec, heads * dv), BF16)]
    out_specs = [pl.BlockSpec((rows, dv), tok),
                 pl.BlockSpec((None, None, None, dk, dv), lambda b, h, t: (layer, b, h, 0, 0)),
                 pl.BlockSpec((n_seq, dv), tok)]
    aliases = {}
    if rs_p_prev is not None:
        in_specs.append(pl.BlockSpec(memory_space=pl.ANY))
        aliases[len(args)] = 1
        args.append(rs_p_prev)
    if rs_s_prev is not None:
        in_specs.append(pl.BlockSpec(memory_space=pl.ANY))
        aliases[len(args)] = 3
        args.append(rs_s_prev)
        out_shape.append(jax.ShapeDtypeStruct(state_ret.shape, state_ret.dtype))
        out_specs.append(state_spec)
    return pl.pallas_call(
        functools.partial(_retention_kernel, chunks_per_step=chunks_per_step,
                          update_state=rs_s_prev is not None),
        out_shape=out_shape,
        grid=(batch, heads, steps),
        in_specs=in_specs,
        out_specs=out_specs,
        input_output_aliases=aliases,
        compiler_params=_params(3),
        name=name,
    )(*args)


SC_STATE_ROWS = 64


def _sample_state_update(state_ret, k_rep, v4, dec_tab, layer, *, name):
    n_ret, n_dec, heads, dk, dv = state_ret.shape
    pairs = n_dec * heads
    info = plsc.get_sparse_core_info()
    lanes = info.num_lanes
    workers = info.num_cores * info.num_subcores
    per_worker = pairs // workers
    rows = SC_STATE_ROWS
    assert per_worker * workers == pairs and dk % rows == 0 and LANES % lanes == 0
    per_row = LANES // lanes
    s3 = state_ret.reshape(n_ret * pairs, dk, dv)
    first = layer * pairs
    mesh = plsc.VectorSubcoreMesh(core_axis_name="core", subcore_axis_name="subcore")

    @pl.kernel(out_type=jax.ShapeDtypeStruct(s3.shape, s3.dtype), mesh=mesh,
               scratch_types=[pltpu.VMEM((rows, dv), F32), pltpu.VMEM((rows, dv), F32),
                              pltpu.VMEM(k_rep.shape[1:], F32), pltpu.VMEM(v4.shape[1:], F32),
                              pltpu.VMEM(dec_tab.shape, F32)],
               name=name)
    def update(s_hbm, k_hbm, v_hbm, dec_hbm, o_hbm, s_buf, o_buf, k_buf, v_buf, dec_buf):
        worker = lax.axis_index("core") * info.num_subcores + lax.axis_index("subcore")
        pltpu.sync_copy(dec_hbm, dec_buf)

        @pl.loop(0, per_worker)
        def _(pi):
            p = worker * per_worker + pi
            h = lax.rem(p, heads)
            pltpu.sync_copy(k_hbm.at[p], k_buf)
            pltpu.sync_copy(v_hbm.at[p], v_buf)
            dec_c = dec_buf[0, h, pl.ds(0, lanes)]
            dec_k = dec_buf[1, h, pl.ds(0, lanes)]

            @pl.loop(0, dk // rows)
            def _(c):
                pltpu.sync_copy(s_hbm.at[first + p, pl.ds(c * rows, rows)], s_buf)

                @plsc.parallel_loop(0, rows, unroll=2)
                def _(r):
                    row = c * rows + r
                    kq = row // per_row
                    kl = pl.multiple_of(lax.rem(row, per_row) * lanes, lanes)
                    kb = k_buf[kq, pl.ds(kl, lanes)] * dec_k
                    for j in range(dv // lanes):
                        lane = pl.ds(j * lanes, lanes)
                        vj = v_buf[j // per_row, pl.ds((j % per_row) * lanes, lanes)]
                        o_buf[r, lane] = dec_c * s_buf[r, lane] + kb * vj

                pltpu.sync_copy(o_buf, o_hbm.at[first + p, pl.ds(c * rows, rows)])

    return update(s3, k_rep, v4, dec_tab).reshape(state_ret.shape)


def _place_rows_kernel(src_ref, dst_ref, o_ref):
    del dst_ref
    o_ref[...] = src_ref[...]


def _place_rows(src, dst, first_row, *, name):
    n, w = src.shape
    return pl.pallas_call(
        _place_rows_kernel,
        out_shape=jax.ShapeDtypeStruct(dst.shape, dst.dtype),
        grid=(1,),
        in_specs=[pl.BlockSpec((n, w), lambda i: (0, 0)), pl.BlockSpec(memory_space=pl.ANY)],
        out_specs=pl.BlockSpec((n, w), lambda i: (first_row // n, 0)),
        input_output_aliases={1: 0},
        compiler_params=_params(1),
        name=name,
    )(src, dst)


def _layer_norm_swish(y, g, b):
    mu = jnp.mean(y, axis=-1, keepdims=True)
    var = jnp.mean(jnp.square(y - mu), axis=-1, keepdims=True)
    z = (y - mu) * lax.rsqrt(var + EPS) * g + b
    return jax.nn.silu(z)


CONV_HALO = 32
CONV_ROW_CHUNK = 32
CONV_LANE_CHUNK = 512


def _conv_prompt_kernel(u_ref, halo_ref, w_ref, bdw_ref, lng_ref, lnb_ref, z_ref, tail_ref,
                        win, shifted, y):
    tt, d = u_ref.shape
    width = w_ref.shape[0]
    hist = width - 1
    lead = CONV_HALO - hist
    n_lane_chunks = d // CONV_LANE_CHUNK
    shift_rows = shifted.shape[1]
    first = pl.program_id(1) == 0
    for lc in range(n_lane_chunks):
        lanes = slice(lc * CONV_LANE_CHUNK, (lc + 1) * CONV_LANE_CHUNK)
        win[lc, 0:CONV_HALO, :] = jnp.where(first, 0.0, halo_ref[:, lanes])
        win[lc, CONV_HALO:, :] = u_ref[:, lanes]

    def per_lane_chunk(lc, carry):
        lane0 = pl.multiple_of(lc * CONV_LANE_CHUNK, CONV_LANE_CHUNK)
        for s in range(1, SUBLANES):
            shifted[s - 1, :, :] = win[lc, s:s + shift_rows, :]
        groups = CONV_ROW_CHUNK // SUBLANES
        for rc in range(tt // CONV_ROW_CHUNK):
            r0 = rc * CONV_ROW_CHUNK
            accs = [None] * groups
            for w in range(width):
                s = (lead + w) % SUBLANES
                base = r0 + lead + w - s
                wv = w_ref[w, :, pl.ds(lane0, CONV_LANE_CHUNK)]
                for rg in range(groups):
                    rows = slice(base + rg * SUBLANES, base + (rg + 1) * SUBLANES)
                    src = win[lc, rows, :] if s == 0 else shifted[s - 1, rows, :]
                    term = src * wv
                    accs[rg] = term if accs[rg] is None else accs[rg] + term
            for rg in range(groups):
                y[lc, r0 + rg * SUBLANES:r0 + (rg + 1) * SUBLANES, :] = accs[rg]
        return carry

    lax.fori_loop(0, n_lane_chunks, per_lane_chunk, 0)

    yy = jnp.concatenate([y[lc] for lc in range(n_lane_chunks)], axis=-1) + bdw_ref[...]
    z_ref[...] = _layer_norm_swish(yy, lng_ref[...], lnb_ref[...]).astype(z_ref.dtype)

    @pl.when(pl.program_id(1) == pl.num_programs(1) - 1)
    def _():
        tail_ref[...] = u_ref[tt - hist:tt, :]


def _conv_prompt(u, w_rep, bdw3, lng3, lnb3, layer, *, batch, seq, tt, name):
    m, d = u.shape
    width = w_rep.shape[1]
    hist = width - 1
    steps = seq // tt
    halo_per_tile = tt // CONV_HALO
    n_lane_chunks = d // CONV_LANE_CHUNK
    vec = pl.BlockSpec((None, 1, d), lambda b, i: (layer, 0, 0))
    return pl.pallas_call(
        _conv_prompt_kernel,
        out_shape=[jax.ShapeDtypeStruct((m, d), BF16),
                   jax.ShapeDtypeStruct((batch, hist, d), F32)],
        grid=(batch, steps),
        in_specs=[pl.BlockSpec((tt, d), lambda b, i: (b * steps + i, 0)),
                  pl.BlockSpec((CONV_HALO, d),
                               lambda b, i: (jnp.maximum((b * steps + i) * halo_per_tile - 1, 0), 0)),
                  pl.BlockSpec((None, width, SUBLANES, d), lambda b, i: (layer, 0, 0, 0)),
                  vec, vec, vec],
        out_specs=[pl.BlockSpec((tt, d), lambda b, i: (b * steps + i, 0)),
                   pl.BlockSpec((None, hist, d), lambda b, i: (b, 0, 0))],
        scratch_shapes=[pltpu.VMEM((n_lane_chunks, tt + CONV_HALO, CONV_LANE_CHUNK), F32),
                        pltpu.VMEM((SUBLANES - 1, tt + CONV_HALO - SUBLANES, CONV_LANE_CHUNK), F32),
                        pltpu.VMEM((n_lane_chunks, tt, CONV_LANE_CHUNK), F32)],
        compiler_params=_params(2),
        name=name,
    )(u, u, w_rep, bdw3, lng3, lnb3)


def _conv_sample_kernel(*refs, aliased_inputs):
    buf_ref, u_ref, w_ref, bdw_ref, lng_ref, lnb_ref = refs[:6]
    z_ref, so_ref = refs[6 + aliased_inputs:]
    hist = buf_ref.shape[0]
    u = u_ref[...]
    acc = u * w_ref[hist:hist + 1, :]
    for w in range(hist):
        acc = acc + buf_ref[w] * w_ref[w:w + 1, :]
    y = acc + bdw_ref[...]
    z_ref[...] = _layer_norm_swish(y, lng_ref[...], lnb_ref[...]).astype(z_ref.dtype)
    for w in range(hist - 1):
        so_ref[w] = buf_ref[w + 1]
    so_ref[hist - 1] = u


def _conv_sample(state_t, u, w_dw, bdw3, lng3, lnb3, z, cs_prev, layer, *, first_row, nb, name):
    hist, n_b, d = state_t.shape[1:]
    width = w_dw.shape[1]
    blk0 = first_row // nb
    vec = pl.BlockSpec((None, 1, d), lambda bb: (layer, 0, 0))
    state_spec = pl.BlockSpec((None, hist, nb, d), lambda bb: (layer, 0, bb, 0))
    in_specs = [state_spec,
                pl.BlockSpec((nb, d), lambda bb: (blk0 + bb, 0)),
                pl.BlockSpec((None, width, d), lambda bb: (layer, 0, 0)),
                vec, vec, vec,
                pl.BlockSpec(memory_space=pl.ANY)]
    args = [state_t, u, w_dw, bdw3, lng3, lnb3, z]
    aliases = {6: 0}
    if cs_prev is not None:
        in_specs.append(pl.BlockSpec(memory_space=pl.ANY))
        args.append(cs_prev)
        aliases[7] = 1
    return pl.pallas_call(
        functools.partial(_conv_sample_kernel, aliased_inputs=len(aliases)),
        out_shape=[jax.ShapeDtypeStruct(z.shape, z.dtype),
                   jax.ShapeDtypeStruct(state_t.shape, state_t.dtype)],
        grid=(n_b // nb,),
        in_specs=in_specs,
        out_specs=[pl.BlockSpec((nb, d), lambda bb: (blk0 + bb, 0)), state_spec],
        input_output_aliases=aliases,
        compiler_params=_params(1),
        name=name,
    )(*args)


def _rope_tables(pos, dk):
    half = dk // 2
    freqs = ROPE_BASE ** (-jnp.arange(half, dtype=F32) / half)
    ang = pos[:, None] * freqs[None, :]
    return jnp.cos(ang), jnp.sin(ang)


def _decay_tables(heads, chunk):
    lg = jnp.log1p(-jnp.exp2(-5.0 - jnp.arange(heads, dtype=F32)))
    idx = jnp.arange(chunk, dtype=F32)
    diff = idx[:, None] - idx[None, :]
    intra = jnp.where(diff[None] >= 0.0,
                      jnp.exp(jnp.maximum(diff, 0.0)[None] * lg[:, None, None]), 0.0)
    dec_q = jnp.exp((idx[:, None] + 1.0) * lg[None, :])
    dec_k = jnp.exp((chunk - 1.0 - idx)[:, None] * lg[None, :])
    dec_c = jnp.exp(chunk * lg)
    return intra, dec_q, dec_k, dec_c


def kernel(x_prompt, x_sample, state_ret, state_conv, norm_mix_g, norm_mlp_g, norm_out_g,
           ret_w_in, ret_gn_g, ret_w_out, conv_w_pw1, conv_b_pw1, conv_w_dw, conv_b_dw,
           conv_ln_g, conv_ln_b, conv_w_pw2, conv_b_pw2, mlp_w1, mlp_w2):
    batch, seq, d = x_prompt.shape
    n_dec = x_sample.shape[0]
    depth = norm_mix_g.shape[0]
    n_ret, _, heads, dk, dv = state_ret.shape
    n_conv = state_conv.shape[0]
    m_p = batch * seq
    m = m_p + n_dec
    ret_qk = heads * dk
    ret_v = heads * dv
    d_ff = mlp_w1.shape[2]
    assert heads == RET_HEADS and x_sample.shape[1] == 1

    tm_d = 1040
    tm_deep = 640
    tm_pw2 = 320
    assert m % tm_d == 0 and m % tm_deep == 0 and m % tm_pw2 == 0 and m_p % n_dec == 0

    cos_p, sin_p = _rope_tables(jnp.arange(seq, dtype=F32), dk)
    cos_s, sin_s = _rope_tables(jnp.arange(1, dtype=F32) + float(PAST_LEN), dk)
    cos = jnp.concatenate([jnp.tile(cos_p, (batch, 1)), jnp.tile(cos_s, (n_dec, 1))], axis=0)
    sin = jnp.concatenate([jnp.tile(sin_p, (batch, 1)), jnp.tile(sin_s, (n_dec, 1))], axis=0)

    chunk = math.gcd(seq, RET_CHUNK)
    intra, dec_q, dec_k, dec_c = _decay_tables(heads, chunk)
    tabs = {
        "intra": intra,
        "dq": jnp.broadcast_to(dec_q.T[:, :, None], (heads, chunk, dv)),
        "dk": jnp.broadcast_to(dec_k.T[:, :, None], (heads, chunk, dk)),
        "dc": jnp.broadcast_to(dec_c[:, None, None], (heads, 1, dv)),
    }
    intra1, dec_q1, dec_k1, dec_c1 = _decay_tables(heads, 1)
    consts1 = jnp.stack([dec_q1[0], dec_c1, intra1[:, 0, 0], dec_k1[0]], axis=1)

    mix_g3 = norm_mix_g.reshape(depth, 1, d)
    mlp_g3 = norm_mlp_g.reshape(depth, 1, d)
    out_g3 = norm_out_g.reshape(1, 1, d)
    gn4 = ret_gn_g.reshape(n_ret, heads, 1, dv)
    b_pw1_3 = conv_b_pw1.reshape(n_conv, 1, 2 * d)
    b_dw3 = conv_b_dw.reshape(n_conv, 1, d)
    ln_g3 = conv_ln_g.reshape(n_conv, 1, d)
    ln_b3 = conv_ln_b.reshape(n_conv, 1, d)
    b_pw2_3 = conv_b_pw2.reshape(n_conv, 1, d)
    state_conv_t = jnp.transpose(state_conv, (0, 2, 1, 3))
    w_dw_rep = jnp.broadcast_to(conv_w_dw[:, :, None, :],
                                (n_conv, conv_w_dw.shape[1], SUBLANES, d))

    rs_p = rs_s = cs_t = None
    conv_tails = []
    x, xg, ssq = _assemble(x_prompt.reshape(m_p, d), x_sample.reshape(n_dec, d), mix_g3, 0,
                           name="assemble")
    for i in range(depth):
        j = i // 2
        if i % 2 == 0:
            qk = _mm_rope(xg, ssq, ret_w_in, j, cos, sin, n_q=ret_qk, tm=tm_d, tn=1024,
                          name=f"ret{j}_qk")
            v = _mm_plain(xg, ssq, ret_w_in, j, n=ret_v, col_offset=2 * ret_qk, tm=tm_d, tn=1024,
                          out_dtype=BF16, relu2=False, name=f"ret{j}_v")
            g = _mm_plain(xg, ssq, ret_w_in, j, n=ret_v, col_offset=2 * ret_qk + ret_v, tm=tm_d,
                          tn=1024, out_dtype=F32, relu2=False, name=f"ret{j}_g")
            if j == 0:
                sc_lanes = plsc.get_sparse_core_info().num_lanes
                k_s = qk[m_p:, ret_qk:].astype(F32).reshape(n_dec * heads, dk)
                k_rep = jnp.repeat(k_s, sc_lanes, axis=1).reshape(n_dec * heads, -1, LANES)
                v_s = v[m_p:].astype(F32).reshape(n_dec * heads, dv // LANES, LANES)
                dec_tab = jnp.broadcast_to(jnp.stack([dec_c1, dec_k1[0]])[:, :, None], (2, heads, LANES))
                rs_s = _sample_state_update(state_ret, k_rep, v_s, dec_tab, j, name="sample_state")
                gated, rs_p, gated_s = _retention(consts1, qk, v, g, tabs, gn4, state_ret, rs_p, None,
                                                  j, batch=batch, seq=seq, chunks_per_step=8,
                                                  name=f"ret{j}_mix")
            else:
                gated, rs_p, gated_s, rs_s = _retention(consts1, qk, v, g, tabs, gn4, state_ret, rs_p,
                                                        rs_s, j, batch=batch, seq=seq,
                                                        chunks_per_step=8, name=f"ret{j}_mix")
            gated = _place_rows(gated_s, gated, m_p, name=f"ret{j}_place")
            x, xg, ssq = _mm_residual(gated, ret_w_out, j, x, tm=tm_deep, tn=1024,
                                      next_gain=(mlp_g3, i), name=f"ret{j}_out")
        else:
            u = _mm_glu(xg, ssq, conv_w_pw1, b_pw1_3, j, tm=tm_d, tn=512, name=f"conv{j}_pw1")
            z, tail = _conv_prompt(u, w_dw_rep, b_dw3, ln_g3, ln_b3, j, batch=batch, seq=seq,
                                   tt=256, name=f"conv{j}_prompt")
            conv_tails.append(tail)
            z, cs_t = _conv_sample(state_conv_t, u, conv_w_dw, b_dw3, ln_g3, ln_b3, z, cs_t, j,
                                   first_row=m_p, nb=16, name=f"conv{j}_sample")
            x, xg, ssq = _mm_residual(z, conv_w_pw2, j, x, tm=tm_pw2, tn=d, bias3=b_pw2_3,
                                      next_gain=(mlp_g3, i), name=f"conv{j}_pw2")
        a = _mm_plain(xg, ssq, mlp_w1, i, n=d_ff, col_offset=0, tm=tm_d, tn=1024, out_dtype=BF16,
                      relu2=True, name=f"mlp{i}_up")
        if i + 1 < depth:
            x, xg, ssq = _mm_residual(a, mlp_w2, i, x, tm=tm_deep, tn=512,
                                      next_gain=(mix_g3, i + 1), name=f"mlp{i}_down")
        else:
            x = _mm_residual(a, mlp_w2, i, x, tm=tm_deep, tn=512, name=f"mlp{i}_down")

    y_prompt = _rmsnorm(x, out_g3, 0, rows=m_p, row_block=512, first_block=0, out_dtype=F32,
                        name="norm_out_prompt")
    y_sample = _rmsnorm(x, out_g3, 0, rows=n_dec, row_block=n_dec, first_block=m_p // n_dec,
                        out_dtype=F32, name="norm_out_sample")
    return (y_prompt.reshape(batch, seq, d), y_sample.reshape(n_dec, 1, d),
            rs_p, rs_s, jnp.stack(conv_tails), jnp.transpose(cs_t, (0, 2, 1, 3)))
```

```python
import functools
import math

import jax
import jax.numpy as jnp
from jax import lax
from jax.experimental import pallas as pl
from jax.experimental.pallas import tpu as pltpu

F32 = jnp.float32
BF16 = jnp.bfloat16

EPS = 1e-6
ROPE_BASE = 10000.0
PAST_LEN = 16384
RET_CHUNK = 128
RET_HEADS = 8

V7X_VMEM_LIMIT_BYTES = 56 * 1024 * 1024
LANES = 128
SUBLANES = 8

WEIGHT_CAST_ROWS = 256
ASSEMBLE_ROWS = 1024


def _params(n_axes):
    return pltpu.CompilerParams(dimension_semantics=("arbitrary",) * n_axes,
                                vmem_limit_bytes=V7X_VMEM_LIMIT_BYTES)


def _rms_scale(x, g):
    ms = jnp.mean(x * x, axis=-1, keepdims=True)
    return x * lax.rsqrt(ms + EPS) * g


def _rmsnorm_kernel(x_ref, g_ref, o_ref):
    o_ref[...] = _rms_scale(x_ref[...], g_ref[...]).astype(o_ref.dtype)


def _rmsnorm(x, g3, layer, *, rows, row_block, first_block, out_dtype, name):
    d = x.shape[1]
    return pl.pallas_call(
        _rmsnorm_kernel,
        out_shape=jax.ShapeDtypeStruct((rows, d), out_dtype),
        grid=(rows // row_block,),
        in_specs=[pl.BlockSpec((row_block, d), lambda i: (i + first_block, 0)),
                  pl.BlockSpec((None, 1, d), lambda i: (layer, 0, 0))],
        out_specs=pl.BlockSpec((row_block, d), lambda i: (i, 0)),
        compiler_params=_params(1),
        name=name,
    )(x, g3)


def _emit_gained(xn, g_ref, xg_ref, ssq_ref):
    xg_ref[...] = (xn * g_ref[...]).astype(xg_ref.dtype)
    sq = xn * xn
    part = sq[:, 0:LANES]
    for t in range(1, xn.shape[1] // LANES):
        part = part + sq[:, t * LANES:(t + 1) * LANES]
    ssq_ref[...] = part


def _row_rms_inv(ssq_ref, width):
    tot = jnp.sum(jnp.sum(ssq_ref[...], axis=0), axis=-1, keepdims=True)
    return lax.rsqrt(tot / width + EPS)


def _assemble_kernel(rows_ref, g_ref, *rest):
    x_ref, xg_ref, ssq_ref = rest[-3:]
    xn = rows_ref[...]
    x_ref[...] = xn
    _emit_gained(xn, g_ref, xg_ref, ssq_ref)


def _assemble(rows_in, g3, layer, prev, *, m, first_row, row_block, name):
    n, d = rows_in.shape
    blk0 = first_row // row_block
    in_specs = [pl.BlockSpec((row_block, d), lambda i: (i, 0)),
                pl.BlockSpec((None, 1, d), lambda i: (layer, 0, 0))]
    args = [rows_in, g3]
    aliases = {}
    if prev is not None:
        in_specs += [pl.BlockSpec(memory_space=pl.ANY)] * 3
        aliases = {2: 0, 3: 1, 4: 2}
        args += list(prev)
    return pl.pallas_call(
        _assemble_kernel,
        out_shape=[jax.ShapeDtypeStruct((m, d), F32), jax.ShapeDtypeStruct((m, d), BF16),
                   jax.ShapeDtypeStruct((1, m, LANES), F32)],
        grid=(n // row_block,),
        in_specs=in_specs,
        out_specs=[pl.BlockSpec((row_block, d), lambda i: (blk0 + i, 0)),
                   pl.BlockSpec((row_block, d), lambda i: (blk0 + i, 0)),
                   pl.BlockSpec((None, row_block, LANES), lambda i: (0, blk0 + i, 0))],
        input_output_aliases=aliases,
        compiler_params=_params(1),
        name=name,
    )(*args)


def _cast_weights_once(w_ref, wb_ref):
    @pl.when(pl.program_id(1) == 0)
    def _():
        def body(r, carry):
            rows = pl.ds(pl.multiple_of(r * WEIGHT_CAST_ROWS, WEIGHT_CAST_ROWS), WEIGHT_CAST_ROWS)
            wb_ref[rows, :] = w_ref[rows, :].astype(BF16)
            return carry
        lax.fori_loop(0, w_ref.shape[0] // WEIGHT_CAST_ROWS, body, 0)


def _mm_rope_kernel(x_ref, ssq_ref, w_ref, cos_ref, sin_ref, o_ref, wb_ref, *, n_q_blocks, k_scale):
    _cast_weights_once(w_ref, wb_ref)
    acc = jnp.dot(x_ref[...], wb_ref[...], preferred_element_type=F32)
    acc = acc * _row_rms_inv(ssq_ref, x_ref.shape[1])
    scale = jnp.where(pl.program_id(0) >= n_q_blocks, k_scale, 1.0).astype(F32)
    cos = cos_ref[...]
    sin = sin_ref[...]
    half = cos.shape[1]
    for hh in range(acc.shape[1] // (2 * half)):
        lo = slice(2 * hh * half, (2 * hh + 1) * half)
        hi = slice((2 * hh + 1) * half, (2 * hh + 2) * half)
        x1 = acc[:, lo]
        x2 = acc[:, hi]
        o_ref[:, lo] = ((x1 * cos - x2 * sin) * scale).astype(o_ref.dtype)
        o_ref[:, hi] = ((x1 * sin + x2 * cos) * scale).astype(o_ref.dtype)


def _mm_plain_kernel(x_ref, ssq_ref, w_ref, o_ref, wb_ref, *, relu2):
    _cast_weights_once(w_ref, wb_ref)
    acc = jnp.dot(x_ref[...], wb_ref[...], preferred_element_type=F32)
    acc = acc * _row_rms_inv(ssq_ref, x_ref.shape[1])
    if relu2:
        acc = jnp.square(jnp.maximum(acc, 0.0))
    o_ref[...] = acc.astype(o_ref.dtype)


def _mm_glu_kernel(x_ref, ssq_ref, wa_ref, wg_ref, ba_ref, bg_ref, o_ref, wab_ref, wgb_ref):
    _cast_weights_once(wa_ref, wab_ref)
    _cast_weights_once(wg_ref, wgb_ref)
    x = x_ref[...]
    r = _row_rms_inv(ssq_ref, x_ref.shape[1])
    a = jnp.dot(x, wab_ref[...], preferred_element_type=F32) * r + ba_ref[...]
    g = jnp.dot(x, wgb_ref[...], preferred_element_type=F32) * r + bg_ref[...]
    o_ref[...] = a * jax.nn.sigmoid(g)


def _mm_residual_kernel(*refs, has_bias, emit):
    refs = list(refs)
    x_ref, w_ref = refs[:2]
    del refs[:2]
    b_ref = refs.pop(0) if has_bias else None
    r_ref = refs.pop(0)
    g_ref = refs.pop(0) if emit else None
    o_ref = refs.pop(0)
    wb_ref = refs.pop()
    _cast_weights_once(w_ref, wb_ref)
    acc = jnp.dot(x_ref[...], wb_ref[...], preferred_element_type=F32)
    if has_bias:
        acc = acc + b_ref[...]
    xn = r_ref[...] + acc
    o_ref[...] = xn
    if emit:
        xg_ref, ssq_ref = refs
        _emit_gained(xn, g_ref, xg_ref, ssq_ref)


def _x_spec(tm, k):
    return pl.BlockSpec((tm, k), lambda j, i: (i, 0))


def _ssq_spec(ssq, tm):
    return pl.BlockSpec((ssq.shape[0], tm, LANES), lambda j, i: (0, i, 0))


def _w_spec(k, tn, layer, col_block_offset=0, single_buffer=False):
    mode = pl.Buffered(1) if single_buffer else None
    return pl.BlockSpec((None, k, tn), lambda j, i: (layer, 0, j + col_block_offset),
                        pipeline_mode=mode)


def _tile_spec(tm, tn):
    return pl.BlockSpec((tm, tn), lambda j, i: (i, j))


def _mm_rope(xg, ssq, w, layer, cos, sin, *, n_q, tm, tn, name):
    m, k = xg.shape
    n_qk = 2 * n_q
    half = cos.shape[1]
    kern = functools.partial(_mm_rope_kernel, n_q_blocks=n_q // tn, k_scale=(2 * half) ** -0.5)
    return pl.pallas_call(
        kern,
        out_shape=jax.ShapeDtypeStruct((m, n_qk), BF16),
        grid=(n_qk // tn, m // tm),
        in_specs=[_x_spec(tm, k), _ssq_spec(ssq, tm), _w_spec(k, tn, layer),
                  pl.BlockSpec((tm, half), lambda j, i: (i, 0)),
                  pl.BlockSpec((tm, half), lambda j, i: (i, 0))],
        out_specs=_tile_spec(tm, tn),
        scratch_shapes=[pltpu.VMEM((k, tn), BF16)],
        compiler_params=_params(2),
        name=name,
    )(xg, ssq, w, cos, sin)


def _mm_plain(xg, ssq, w, layer, *, n, col_offset, tm, tn, out_dtype, relu2, name):
    m, k = xg.shape
    return pl.pallas_call(
        functools.partial(_mm_plain_kernel, relu2=relu2),
        out_shape=jax.ShapeDtypeStruct((m, n), out_dtype),
        grid=(n // tn, m // tm),
        in_specs=[_x_spec(tm, k), _ssq_spec(ssq, tm), _w_spec(k, tn, layer, col_offset // tn)],
        out_specs=_tile_spec(tm, tn),
        scratch_shapes=[pltpu.VMEM((k, tn), BF16)],
        compiler_params=_params(2),
        name=name,
    )(xg, ssq, w)


def _mm_glu(xg, ssq, w, b3, layer, *, tm, tn, name):
    m, k = xg.shape
    n = w.shape[2] // 2
    gate = n // tn
    return pl.pallas_call(
        _mm_glu_kernel,
        out_shape=jax.ShapeDtypeStruct((m, n), F32),
        grid=(n // tn, m // tm),
        in_specs=[_x_spec(tm, k), _ssq_spec(ssq, tm),
                  _w_spec(k, tn, layer), _w_spec(k, tn, layer, gate),
                  pl.BlockSpec((None, 1, tn), lambda j, i: (layer, 0, j)),
                  pl.BlockSpec((None, 1, tn), lambda j, i: (layer, 0, j + gate))],
        out_specs=_tile_spec(tm, tn),
        scratch_shapes=[pltpu.VMEM((k, tn), BF16), pltpu.VMEM((k, tn), BF16)],
        compiler_params=_params(2),
        name=name,
    )(xg, ssq, w, w, b3, b3)


def _mm_residual(a, w, layer, res, *, tm, tn, name, bias3=None, next_gain=None):
    m, k = a.shape
    n = w.shape[2]
    in_specs = [_x_spec(tm, k), _w_spec(k, tn, layer, single_buffer=True)]
    args = [a, w]
    if bias3 is not None:
        in_specs.append(pl.BlockSpec((None, 1, tn), lambda j, i: (layer, 0, j)))
        args.append(bias3)
    in_specs.append(_tile_spec(tm, tn))
    args.append(res)
    out_shape = [jax.ShapeDtypeStruct((m, n), F32)]
    out_specs = [_tile_spec(tm, tn)]
    if next_gain is not None:
        g3, norm_layer = next_gain
        in_specs.append(pl.BlockSpec((None, 1, tn), lambda j, i: (norm_layer, 0, j)))
        args.append(g3)
        out_shape += [jax.ShapeDtypeStruct((m, n), BF16),
                      jax.ShapeDtypeStruct((n // tn, m, LANES), F32)]
        out_specs += [_tile_spec(tm, tn), pl.BlockSpec((None, tm, LANES), lambda j, i: (j, i, 0))]
    out = pl.pallas_call(
        functools.partial(_mm_residual_kernel, has_bias=bias3 is not None,
                          emit=next_gain is not None),
        out_shape=out_shape,
        grid=(n // tn, m // tm),
        in_specs=in_specs,
        out_specs=out_specs,
        scratch_shapes=[pltpu.VMEM((k, tn), BF16)],
        compiler_params=_params(2),
        name=name,
    )(*args)
    return out if next_gain is not None else out[0]


def _group_norm_gate(o, g, gn):
    mu = jnp.mean(o, axis=-1, keepdims=True)
    var = jnp.mean(jnp.square(o - mu), axis=-1, keepdims=True)
    on = (o - mu) * lax.rsqrt(var + EPS) * gn
    return jax.nn.silu(g) * on


def _ret_sample_step(c_ref, q_ref, k_ref, v_ref, g_ref, gn, st_ref, o_ref, so_ref, blk):
    h = pl.program_id(1)
    dec_q = c_ref[h, 0]
    dec_c = c_ref[h, 1]
    intra = c_ref[h, 2]
    dec_k = c_ref[h, 3]
    n_seq = o_ref.shape[0]
    dk = q_ref.shape[1]
    dv = v_ref.shape[1]
    rows = pl.ds(pl.multiple_of(blk * n_seq, n_seq), n_seq)
    q_rows = q_ref[rows, :].astype(F32)
    k_rows = k_ref[rows, :].astype(F32)
    v_rows = v_ref[rows, :].astype(F32)
    o_rows = []
    for bl in range(n_seq):
        qrow = q_rows[bl:bl + 1, :]
        krow = k_rows[bl:bl + 1, :]
        vrow = v_rows[bl:bl + 1, :]
        qcol = jnp.broadcast_to(qrow, (LANES, dk)).T
        kcol = jnp.broadcast_to(krow * dec_k, (LANES, dk)).T
        qk = jnp.sum(qrow * krow, axis=-1, keepdims=True) * intra
        o_tiles = []
        for t in range(dv // LANES):
            lanes = slice(t * LANES, (t + 1) * LANES)
            st = st_ref[bl, :, lanes]
            vt = vrow[:, lanes]
            inter = jnp.sum(qcol * st, axis=0, keepdims=True)
            o_tiles.append(qk * vt + inter * dec_q)
            so_ref[bl, :, lanes] = dec_c * st + kcol * vt
        o_rows.append(jnp.concatenate(o_tiles, axis=1))
    o = jnp.concatenate(o_rows, axis=0)
    o_ref[...] = _group_norm_gate(o, g_ref[rows, :], gn).astype(o_ref.dtype)


def _ret_prompt_step(q_ref, k_ref, v_ref, g_ref, intra_ref, dq_ref, dk_ref, dc_ref, gn, o_ref, s_ref,
                     chunks_per_step):
    @pl.when(pl.program_id(2) == 0)
    def _():
        s_ref[...] = jnp.zeros_like(s_ref)

    c = intra_ref.shape[0]
    s = s_ref[...]
    for cc in range(chunks_per_step):
        rows = slice(cc * c, (cc + 1) * c)
        q = q_ref[rows, :]
        k = k_ref[rows, :]
        v = v_ref[rows, :]
        sc = lax.dot_general(q, k, (((1,), (1,)), ((), ())), preferred_element_type=F32)
        sc = (sc * intra_ref[...]).astype(BF16)
        o = (jnp.dot(sc, v, preferred_element_type=F32)
             + jnp.dot(q, s.astype(BF16), preferred_element_type=F32) * dq_ref[...])
        kd_t = (k.astype(F32) * dk_ref[...]).T.astype(BF16)
        s = dc_ref[...] * s + jnp.dot(kd_t, v, preferred_element_type=F32)
        o_ref[rows, :] = _group_norm_gate(o, g_ref[rows, :], gn).astype(o_ref.dtype)
    s_ref[...] = s


def _retention_kernel(c_ref, q_ref, k_ref, v_ref, g_ref, intra_ref, dq_ref, dk_ref, dc_ref, gn_ref,
                      qs_ref, ks_ref, vs_ref, gs_ref, st_ref, *rest, chunks_per_step):
    o_ref, s_ref, os_ref, so_ref = rest[-4:]
    gn = gn_ref[...]
    blk = pl.program_id(0) * pl.num_programs(2) + pl.program_id(2)
    _ret_sample_step(c_ref, qs_ref, ks_ref, vs_ref, gs_ref, gn, st_ref, os_ref, so_ref, blk)
    _ret_prompt_step(q_ref, k_ref, v_ref, g_ref, intra_ref, dq_ref, dk_ref, dc_ref, gn, o_ref, s_ref,
                     chunks_per_step)


def _retention(consts, qk, v, g, tabs, gn4, state_ret, rs_p_prev, rs_s_prev, layer, *,
               batch, seq, chunks_per_step, name):
    m = qk.shape[0]
    heads = RET_HEADS
    n_ret, n_dec, _, dk, dv = state_ret.shape
    c = RET_CHUNK
    rows = c * chunks_per_step
    steps = seq // rows
    n_seq = n_dec // (batch * steps)
    sample_blk = (batch * seq) // n_dec
    assert n_seq * batch * steps == n_dec and n_seq % 16 == 0

    def tok(b, h, t):
        return (b * steps + t, h)

    def sample_tok(b, h, t):
        return (sample_blk, h)

    state_spec = pl.BlockSpec((None, n_seq, None, dk, dv), lambda b, h, t: (layer, b * steps + t, h, 0, 0))
    in_specs = [
        pl.BlockSpec(memory_space=pltpu.SMEM),
        pl.BlockSpec((rows, dk), tok),
        pl.BlockSpec((rows, dk), lambda b, h, t: (b * steps + t, heads + h)),
        pl.BlockSpec((rows, dv), tok),
        pl.BlockSpec((rows, dv), tok),
        pl.BlockSpec((None, c, c), lambda b, h, t: (h, 0, 0)),
        pl.BlockSpec((None, c, dv), lambda b, h, t: (h, 0, 0)),
        pl.BlockSpec((None, c, dk), lambda b, h, t: (h, 0, 0)),
        pl.BlockSpec((None, 1, dv), lambda b, h, t: (h, 0, 0)),
        pl.BlockSpec((None, None, 1, dv), lambda b, h, t: (layer, h, 0, 0)),
        pl.BlockSpec((n_dec, dk), sample_tok),
        pl.BlockSpec((n_dec, dk), lambda b, h, t: (sample_blk, heads + h)),
        pl.BlockSpec((n_dec, dv), sample_tok),
        pl.BlockSpec((n_dec, dv), sample_tok),
        state_spec,
    ]
    args = [consts, qk, qk, v, g, tabs["intra"], tabs["dq"], tabs["dk"], tabs["dc"], gn4,
            qk, qk, v, g, state_ret]
    aliases = {}
    if rs_p_prev is not None:
        in_specs += [pl.BlockSpec(memory_space=pl.ANY), pl.BlockSpec(memory_space=pl.ANY)]
        aliases = {len(args): 1, len(args) + 1: 3}
        args += [rs_p_prev, rs_s_prev]
    return pl.pallas_call(
        functools.partial(_retention_kernel, chunks_per_step=chunks_per_step),
        out_shape=[jax.ShapeDtypeStruct((m, heads * dv), BF16),
                   jax.ShapeDtypeStruct((n_ret, batch, heads, dk, dv), F32),
                   jax.ShapeDtypeStruct((n_dec, heads * dv), BF16),
                   jax.ShapeDtypeStruct(state_ret.shape, state_ret.dtype)],
        grid=(batch, heads, steps),
        in_specs=in_specs,
        out_specs=[pl.BlockSpec((rows, dv), tok),
                   pl.BlockSpec((None, None, None, dk, dv), lambda b, h, t: (layer, b, h, 0, 0)),
                   pl.BlockSpec((n_seq, dv), tok),
                   state_spec],
        input_output_aliases=aliases,
        compiler_params=_params(3),
        name=name,
    )(*args)


def _place_rows_kernel(src_ref, dst_ref, o_ref):
    del dst_ref
    o_ref[...] = src_ref[...]


def _place_rows(src, dst, first_row, *, name):
    n, w = src.shape
    return pl.pallas_call(
        _place_rows_kernel,
        out_shape=jax.ShapeDtypeStruct(dst.shape, dst.dtype),
        grid=(1,),
        in_specs=[pl.BlockSpec((n, w), lambda i: (0, 0)), pl.BlockSpec(memory_space=pl.ANY)],
        out_specs=pl.BlockSpec((n, w), lambda i: (first_row // n, 0)),
        input_output_aliases={1: 0},
        compiler_params=_params(1),
        name=name,
    )(src, dst)


def _layer_norm_swish(y, g, b):
    mu = jnp.mean(y, axis=-1, keepdims=True)
    var = jnp.mean(jnp.square(y - mu), axis=-1, keepdims=True)
    z = (y - mu) * lax.rsqrt(var + EPS) * g + b
    return jax.nn.silu(z)


CONV_HALO = 32
CONV_ROW_CHUNK = 32
CONV_LANE_CHUNK = 512


def _conv_prompt_kernel(u_ref, halo_ref, w_ref, bdw_ref, lng_ref, lnb_ref, z_ref, tail_ref,
                        win, shifted, y):
    tt, d = u_ref.shape
    width = w_ref.shape[0]
    hist = width - 1
    lead = CONV_HALO - hist
    n_lane_chunks = d // CONV_LANE_CHUNK
    shift_rows = shifted.shape[1]
    first = pl.program_id(1) == 0
    for lc in range(n_lane_chunks):
        lanes = slice(lc * CONV_LANE_CHUNK, (lc + 1) * CONV_LANE_CHUNK)
        win[lc, 0:CONV_HALO, :] = jnp.where(first, 0.0, halo_ref[:, lanes])
        win[lc, CONV_HALO:, :] = u_ref[:, lanes]

    def per_lane_chunk(lc, carry):
        lane0 = pl.multiple_of(lc * CONV_LANE_CHUNK, CONV_LANE_CHUNK)
        for s in range(1, SUBLANES):
            shifted[s - 1, :, :] = win[lc, s:s + shift_rows, :]
        groups = CONV_ROW_CHUNK // SUBLANES
        for rc in range(tt // CONV_ROW_CHUNK):
            r0 = rc * CONV_ROW_CHUNK
            accs = [None] * groups
            for w in range(width):
                s = (lead + w) % SUBLANES
                base = r0 + lead + w - s
                wv = w_ref[w, :, pl.ds(lane0, CONV_LANE_CHUNK)]
                for rg in range(groups):
                    rows = slice(base + rg * SUBLANES, base + (rg + 1) * SUBLANES)
                    src = win[lc, rows, :] if s == 0 else shifted[s - 1, rows, :]
                    term = src * wv
                    accs[rg] = term if accs[rg] is None else accs[rg] + term
            for rg in range(groups):
                y[lc, r0 + rg * SUBLANES:r0 + (rg + 1) * SUBLANES, :] = accs[rg]
        return carry

    lax.fori_loop(0, n_lane_chunks, per_lane_chunk, 0)

    yy = jnp.concatenate([y[lc] for lc in range(n_lane_chunks)], axis=-1) + bdw_ref[...]
    z_ref[...] = _layer_norm_swish(yy, lng_ref[...], lnb_ref[...]).astype(z_ref.dtype)

    @pl.when(pl.program_id(1) == pl.num_programs(1) - 1)
    def _():
        tail_ref[...] = u_ref[tt - hist:tt, :]


def _conv_prompt(u, w_rep, bdw3, lng3, lnb3, layer, *, batch, seq, tt, name):
    m, d = u.shape
    width = w_rep.shape[1]
    hist = width - 1
    steps = seq // tt
    halo_per_tile = tt // CONV_HALO
    n_lane_chunks = d // CONV_LANE_CHUNK
    vec = pl.BlockSpec((None, 1, d), lambda b, i: (layer, 0, 0))
    return pl.pallas_call(
        _conv_prompt_kernel,
        out_shape=[jax.ShapeDtypeStruct((m, d), BF16),
                   jax.ShapeDtypeStruct((batch, hist, d), F32)],
        grid=(batch, steps),
        in_specs=[pl.BlockSpec((tt, d), lambda b, i: (b * steps + i, 0)),
                  pl.BlockSpec((CONV_HALO, d),
                               lambda b, i: (jnp.maximum((b * steps + i) * halo_per_tile - 1, 0), 0)),
                  pl.BlockSpec((None, width, SUBLANES, d), lambda b, i: (layer, 0, 0, 0)),
                  vec, vec, vec],
        out_specs=[pl.BlockSpec((tt, d), lambda b, i: (b * steps + i, 0)),
                   pl.BlockSpec((None, hist, d), lambda b, i: (b, 0, 0))],
        scratch_shapes=[pltpu.VMEM((n_lane_chunks, tt + CONV_HALO, CONV_LANE_CHUNK), F32),
                        pltpu.VMEM((SUBLANES - 1, tt + CONV_HALO - SUBLANES, CONV_LANE_CHUNK), F32),
                        pltpu.VMEM((n_lane_chunks, tt, CONV_LANE_CHUNK), F32)],
        compiler_params=_params(2),
        name=name,
    )(u, u, w_rep, bdw3, lng3, lnb3)


def _conv_sample_kernel(*refs, aliased_inputs):
    buf_ref, u_ref, w_ref, bdw_ref, lng_ref, lnb_ref = refs[:6]
    z_ref, so_ref = refs[6 + aliased_inputs:]
    hist = buf_ref.shape[0]
    u = u_ref[...]
    acc = u * w_ref[hist:hist + 1, :]
    for w in range(hist):
        acc = acc + buf_ref[w] * w_ref[w:w + 1, :]
    y = acc + bdw_ref[...]
    z_ref[...] = _layer_norm_swish(y, lng_ref[...], lnb_ref[...]).astype(z_ref.dtype)
    for w in range(hist - 1):
        so_ref[w] = buf_ref[w + 1]
    so_ref[hist - 1] = u


def _conv_sample(state_t, u, w_dw, bdw3, lng3, lnb3, z, cs_prev, layer, *, first_row, nb, name):
    hist, n_b, d = state_t.shape[1:]
    width = w_dw.shape[1]
    blk0 = first_row // nb
    vec = pl.BlockSpec((None, 1, d), lambda bb: (layer, 0, 0))
    state_spec = pl.BlockSpec((None, hist, nb, d), lambda bb: (layer, 0, bb, 0))
    in_specs = [state_spec,
                pl.BlockSpec((nb, d), lambda bb: (blk0 + bb, 0)),
                pl.BlockSpec((None, width, d), lambda bb: (layer, 0, 0)),
                vec, vec, vec,
                pl.BlockSpec(memory_space=pl.ANY)]
    args = [state_t, u, w_dw, bdw3, lng3, lnb3, z]
    aliases = {6: 0}
    if cs_prev is not None:
        in_specs.append(pl.BlockSpec(memory_space=pl.ANY))
        args.append(cs_prev)
        aliases[7] = 1
    return pl.pallas_call(
        functools.partial(_conv_sample_kernel, aliased_inputs=len(aliases)),
        out_shape=[jax.ShapeDtypeStruct(z.shape, z.dtype),
                   jax.ShapeDtypeStruct(state_t.shape, state_t.dtype)],
        grid=(n_b // nb,),
        in_specs=in_specs,
        out_specs=[pl.BlockSpec((nb, d), lambda bb: (blk0 + bb, 0)), state_spec],
        input_output_aliases=aliases,
        compiler_params=_params(1),
        name=name,
    )(*args)


def _rope_tables(pos, dk):
    half = dk // 2
    freqs = ROPE_BASE ** (-jnp.arange(half, dtype=F32) / half)
    ang = pos[:, None] * freqs[None, :]
    return jnp.cos(ang), jnp.sin(ang)


def _decay_tables(heads, chunk):
    lg = jnp.log1p(-jnp.exp2(-5.0 - jnp.arange(heads, dtype=F32)))
    idx = jnp.arange(chunk, dtype=F32)
    diff = idx[:, None] - idx[None, :]
    intra = jnp.where(diff[None] >= 0.0,
                      jnp.exp(jnp.maximum(diff, 0.0)[None] * lg[:, None, None]), 0.0)
    dec_q = jnp.exp((idx[:, None] + 1.0) * lg[None, :])
    dec_k = jnp.exp((chunk - 1.0 - idx)[:, None] * lg[None, :])
    dec_c = jnp.exp(chunk * lg)
    return intra, dec_q, dec_k, dec_c


def kernel(x_prompt, x_sample, state_ret, state_conv, norm_mix_g, norm_mlp_g, norm_out_g,
           ret_w_in, ret_gn_g, ret_w_out, conv_w_pw1, conv_b_pw1, conv_w_dw, conv_b_dw,
           conv_ln_g, conv_ln_b, conv_w_pw2, conv_b_pw2, mlp_w1, mlp_w2):
    batch, seq, d = x_prompt.shape
    n_dec = x_sample.shape[0]
    depth = norm_mix_g.shape[0]
    n_ret, _, heads, dk, dv = state_ret.shape
    n_conv = state_conv.shape[0]
    m_p = batch * seq
    m = m_p + n_dec
    ret_qk = heads * dk
    ret_v = heads * dv
    d_ff = mlp_w1.shape[2]
    assert heads == RET_HEADS and x_sample.shape[1] == 1

    tm_d = 1040
    tm_deep = 640
    tm_pw2 = 320
    assert m % tm_d == 0 and m % tm_deep == 0 and m % tm_pw2 == 0
    assert m_p % n_dec == 0 and m_p % ASSEMBLE_ROWS == 0

    cos_p, sin_p = _rope_tables(jnp.arange(seq, dtype=F32), dk)
    cos_s, sin_s = _rope_tables(jnp.arange(1, dtype=F32) + float(PAST_LEN), dk)
    cos = jnp.concatenate([jnp.tile(cos_p, (batch, 1)), jnp.tile(cos_s, (n_dec, 1))], axis=0)
    sin = jnp.concatenate([jnp.tile(sin_p, (batch, 1)), jnp.tile(sin_s, (n_dec, 1))], axis=0)

    chunk = math.gcd(seq, RET_CHUNK)
    intra, dec_q, dec_k, dec_c = _decay_tables(heads, chunk)
    tabs = {
        "intra": intra,
        "dq": jnp.broadcast_to(dec_q.T[:, :, None], (heads, chunk, dv)),
        "dk": jnp.broadcast_to(dec_k.T[:, :, None], (heads, chunk, dk)),
        "dc": jnp.broadcast_to(dec_c[:, None, None], (heads, 1, dv)),
    }
    intra1, dec_q1, dec_k1, dec_c1 = _decay_tables(heads, 1)
    consts1 = jnp.stack([dec_q1[0], dec_c1, intra1[:, 0, 0], dec_k1[0]], axis=1)

    mix_g3 = norm_mix_g.reshape(depth, 1, d)
    mlp_g3 = norm_mlp_g.reshape(depth, 1, d)
    out_g3 = norm_out_g.reshape(1, 1, d)
    gn4 = ret_gn_g.reshape(n_ret, heads, 1, dv)
    b_pw1_3 = conv_b_pw1.reshape(n_conv, 1, 2 * d)
    b_dw3 = conv_b_dw.reshape(n_conv, 1, d)
    ln_g3 = conv_ln_g.reshape(n_conv, 1, d)
    ln_b3 = conv_ln_b.reshape(n_conv, 1, d)
    b_pw2_3 = conv_b_pw2.reshape(n_conv, 1, d)
    state_conv_t = jnp.transpose(state_conv, (0, 2, 1, 3))
    w_dw_rep = jnp.broadcast_to(conv_w_dw[:, :, None, :],
                                (n_conv, conv_w_dw.shape[1], SUBLANES, d))

    rs_p = rs_s = cs_t = None
    conv_tails = []
    stream = _assemble(x_prompt.reshape(m_p, d), mix_g3, 0, None, m=m, first_row=0,
                       row_block=ASSEMBLE_ROWS, name="assemble_prompt")
    x, xg, ssq = _assemble(x_sample.reshape(n_dec, d), mix_g3, 0, stream, m=m, first_row=m_p,
                           row_block=n_dec, name="assemble_sample")
    for i in range(depth):
        j = i // 2
        if i % 2 == 0:
            qk = _mm_rope(xg, ssq, ret_w_in, j, cos, sin, n_q=ret_qk, tm=tm_d, tn=1024,
                          name=f"ret{j}_qk")
            v = _mm_plain(xg, ssq, ret_w_in, j, n=ret_v, col_offset=2 * ret_qk, tm=tm_d, tn=1024,
                          out_dtype=BF16, relu2=False, name=f"ret{j}_v")
            g = _mm_plain(xg, ssq, ret_w_in, j, n=ret_v, col_offset=2 * ret_qk + ret_v, tm=tm_d,
                          tn=1024, out_dtype=F32, relu2=False, name=f"ret{j}_g")
            gated, rs_p, gated_s, rs_s = _retention(consts1, qk, v, g, tabs, gn4, state_ret, rs_p, rs_s,
                                                    j, batch=batch, seq=seq, chunks_per_step=8,
                                                    name=f"ret{j}_mix")
            gated = _place_rows(gated_s, gated, m_p, name=f"ret{j}_place")
            x, xg, ssq = _mm_residual(gated, ret_w_out, j, x, tm=tm_deep, tn=1024,
                                      next_gain=(mlp_g3, i), name=f"ret{j}_out")
        else:
            u = _mm_glu(xg, ssq, conv_w_pw1, b_pw1_3, j, tm=tm_d, tn=512, name=f"conv{j}_pw1")
            z, tail = _conv_prompt(u, w_dw_rep, b_dw3, ln_g3, ln_b3, j, batch=batch, seq=seq,
                                   tt=256, name=f"conv{j}_prompt")
            conv_tails.append(tail)
            z, cs_t = _conv_sample(state_conv_t, u, conv_w_dw, b_dw3, ln_g3, ln_b3, z, cs_t, j,
                                   first_row=m_p, nb=16, name=f"conv{j}_sample")
            x, xg, ssq = _mm_residual(z, conv_w_pw2, j, x, tm=tm_pw2, tn=d, bias3=b_pw2_3,
                                      next_gain=(mlp_g3, i), name=f"conv{j}_pw2")
        a = _mm_plain(xg, ssq, mlp_w1, i, n=d_ff, col_offset=0, tm=tm_d, tn=1024, out_dtype=BF16,
                      relu2=True, name=f"mlp{i}_up")
        if i + 1 < depth:
            x, xg, ssq = _mm_residual(a, mlp_w2, i, x, tm=tm_deep, tn=512,
                                      next_gain=(mix_g3, i + 1), name=f"mlp{i}_down")
        else:
            x = _mm_residual(a, mlp_w2, i, x, tm=tm_deep, tn=512, name=f"mlp{i}_down")

    y_prompt = _rmsnorm(x, out_g3, 0, rows=m_p, row_block=512, first_block=0, out_dtype=F32,
                        name="norm_out_prompt")
    y_sample = _rmsnorm(x, out_g3, 0, rows=n_dec, row_block=n_dec, first_block=m_p // n_dec,
                        out_dtype=F32, name="norm_out_sample")
    return (y_prompt.reshape(batch, seq, d), y_sample.reshape(n_dec, 1, d),
            rs_p, rs_s, jnp.stack(conv_tails), jnp.transpose(cs_t, (0, 2, 1, 3)))
```

```python
import functools
import math

import jax
import jax.numpy as jnp
from jax import lax
from jax.experimental import pallas as pl
from jax.experimental.pallas import tpu as pltpu

F32 = jnp.float32
BF16 = jnp.bfloat16

EPS = 1e-6
ROPE_BASE = 10000.0
PAST_LEN = 16384
RET_CHUNK = 128
RET_HEADS = 8

V7X_VMEM_LIMIT_BYTES = 56 * 1024 * 1024
LANES = 128
SUBLANES = 8

WEIGHT_CAST_ROWS = 256


def _params(n_axes):
    return pltpu.CompilerParams(dimension_semantics=("arbitrary",) * n_axes,
                                vmem_limit_bytes=V7X_VMEM_LIMIT_BYTES)


def _rms_scale(x, g):
    ms = jnp.mean(x * x, axis=-1, keepdims=True)
    return x * lax.rsqrt(ms + EPS) * g


def _rmsnorm_kernel(x_ref, g_ref, o_ref):
    o_ref[...] = _rms_scale(x_ref[...], g_ref[...]).astype(o_ref.dtype)


def _rmsnorm(x, g3, layer, *, rows, row_block, first_block, out_dtype, name):
    d = x.shape[1]
    return pl.pallas_call(
        _rmsnorm_kernel,
        out_shape=jax.ShapeDtypeStruct((rows, d), out_dtype),
        grid=(rows // row_block,),
        in_specs=[pl.BlockSpec((row_block, d), lambda i: (i + first_block, 0)),
                  pl.BlockSpec((None, 1, d), lambda i: (layer, 0, 0))],
        out_specs=pl.BlockSpec((row_block, d), lambda i: (i, 0)),
        compiler_params=_params(1),
        name=name,
    )(x, g3)


def _emit_gained(xn, g_ref, xg_ref, ssq_ref):
    xg_ref[...] = (xn * g_ref[...]).astype(xg_ref.dtype)
    sq = xn * xn
    part = sq[:, 0:LANES]
    for t in range(1, xn.shape[1] // LANES):
        part = part + sq[:, t * LANES:(t + 1) * LANES]
    ssq_ref[...] = part


def _row_rms_inv(ssq_ref, width):
    tot = jnp.sum(jnp.sum(ssq_ref[...], axis=0), axis=-1, keepdims=True)
    return lax.rsqrt(tot / width + EPS)


def _assemble_kernel(xp_ref, xs_ref, g_ref, x_ref, xg_ref, ssq_ref, *, n_prompt_blocks):
    xn = jnp.where(pl.program_id(0) < n_prompt_blocks, xp_ref[...], xs_ref[...])
    x_ref[...] = xn
    _emit_gained(xn, g_ref, xg_ref, ssq_ref)


def _assemble(xp, xs, g3, layer, *, name):
    m_p, d = xp.shape
    rb = xs.shape[0]
    m = m_p + rb
    n_prompt_blocks = m_p // rb
    return pl.pallas_call(
        functools.partial(_assemble_kernel, n_prompt_blocks=n_prompt_blocks),
        out_shape=[jax.ShapeDtypeStruct((m, d), F32), jax.ShapeDtypeStruct((m, d), BF16),
                   jax.ShapeDtypeStruct((1, m, LANES), F32)],
        grid=(m // rb,),
        in_specs=[pl.BlockSpec((rb, d), lambda i: (jnp.minimum(i, n_prompt_blocks - 1), 0)),
                  pl.BlockSpec((rb, d), lambda i: (0, 0)),
                  pl.BlockSpec((None, 1, d), lambda i: (layer, 0, 0))],
        out_specs=[pl.BlockSpec((rb, d), lambda i: (i, 0)), pl.BlockSpec((rb, d), lambda i: (i, 0)),
                   pl.BlockSpec((None, rb, LANES), lambda i: (0, i, 0))],
        compiler_params=_params(1),
        name=name,
    )(xp, xs, g3)


def _cast_weights_once(w_ref, wb_ref):
    @pl.when(pl.program_id(1) == 0)
    def _():
        def body(r, carry):
            rows = pl.ds(pl.multiple_of(r * WEIGHT_CAST_ROWS, WEIGHT_CAST_ROWS), WEIGHT_CAST_ROWS)
            wb_ref[rows, :] = w_ref[rows, :].astype(BF16)
            return carry
        lax.fori_loop(0, w_ref.shape[0] // WEIGHT_CAST_ROWS, body, 0)


def _mm_rope_kernel(x_ref, ssq_ref, w_ref, cos_ref, sin_ref, o_ref, wb_ref, *, n_q_blocks, k_scale):
    _cast_weights_once(w_ref, wb_ref)
    acc = jnp.dot(x_ref[...], wb_ref[...], preferred_element_type=F32)
    acc = acc * _row_rms_inv(ssq_ref, x_ref.shape[1])
    scale = jnp.where(pl.program_id(0) >= n_q_blocks, k_scale, 1.0).astype(F32)
    cos = cos_ref[...]
    sin = sin_ref[...]
    half = cos.shape[1]
    for hh in range(acc.shape[1] // (2 * half)):
        lo = slice(2 * hh * half, (2 * hh + 1) * half)
        hi = slice((2 * hh + 1) * half, (2 * hh + 2) * half)
        x1 = acc[:, lo]
        x2 = acc[:, hi]
        o_ref[:, lo] = ((x1 * cos - x2 * sin) * scale).astype(o_ref.dtype)
        o_ref[:, hi] = ((x1 * sin + x2 * cos) * scale).astype(o_ref.dtype)


def _mm_plain_kernel(x_ref, ssq_ref, w_ref, o_ref, wb_ref, *, relu2):
    _cast_weights_once(w_ref, wb_ref)
    acc = jnp.dot(x_ref[...], wb_ref[...], preferred_element_type=F32)
    acc = acc * _row_rms_inv(ssq_ref, x_ref.shape[1])
    if relu2:
        acc = jnp.square(jnp.maximum(acc, 0.0))
    o_ref[...] = acc.astype(o_ref.dtype)


def _mm_glu_kernel(x_ref, ssq_ref, wa_ref, wg_ref, ba_ref, bg_ref, o_ref, wab_ref, wgb_ref):
    _cast_weights_once(wa_ref, wab_ref)
    _cast_weights_once(wg_ref, wgb_ref)
    x = x_ref[...]
    r = _row_rms_inv(ssq_ref, x_ref.shape[1])
    a = jnp.dot(x, wab_ref[...], preferred_element_type=F32) * r + ba_ref[...]
    g = jnp.dot(x, wgb_ref[...], preferred_element_type=F32) * r + bg_ref[...]
    o_ref[...] = a * jax.nn.sigmoid(g)


def _stage_weights(w_hbm, layer, w_stage, wb_ref, sem):
    j = pl.program_id(0)
    tn = w_stage.shape[1]

    def tile_copy(jj):
        cols = pl.ds(pl.multiple_of(jj * tn, tn), tn)
        return pltpu.make_async_copy(w_hbm.at[layer, :, cols], w_stage, sem)

    @pl.when(pl.program_id(1) == 0)
    def _():
        @pl.when(j == 0)
        def _():
            tile_copy(0).start()

        tile_copy(j).wait()

        def body(r, carry):
            rows = pl.ds(pl.multiple_of(r * WEIGHT_CAST_ROWS, WEIGHT_CAST_ROWS), WEIGHT_CAST_ROWS)
            wb_ref[rows, :] = w_stage[rows, :].astype(BF16)
            return carry
        lax.fori_loop(0, w_stage.shape[0] // WEIGHT_CAST_ROWS, body, 0)

        @pl.when(j + 1 < pl.num_programs(0))
        def _():
            tile_copy(j + 1).start()


def _mm_residual_kernel(*refs, layer, has_bias, emit):
    refs = list(refs)
    x_ref, w_hbm = refs[:2]
    del refs[:2]
    b_ref = refs.pop(0) if has_bias else None
    r_ref = refs.pop(0)
    g_ref = refs.pop(0) if emit else None
    o_ref = refs.pop(0)
    wb_ref, w_stage, sem = refs[-3:]
    del refs[-3:]
    _stage_weights(w_hbm, layer, w_stage, wb_ref, sem)
    acc = jnp.dot(x_ref[...], wb_ref[...], preferred_element_type=F32)
    if has_bias:
        acc = acc + b_ref[...]
    xn = r_ref[...] + acc
    o_ref[...] = xn
    if emit:
        xg_ref, ssq_ref = refs
        _emit_gained(xn, g_ref, xg_ref, ssq_ref)


def _x_spec(tm, k):
    return pl.BlockSpec((tm, k), lambda j, i: (i, 0))


def _ssq_spec(ssq, tm):
    return pl.BlockSpec((ssq.shape[0], tm, LANES), lambda j, i: (0, i, 0))


def _w_spec(k, tn, layer, col_block_offset=0):
    return pl.BlockSpec((None, k, tn), lambda j, i: (layer, 0, j + col_block_offset))


def _tile_spec(tm, tn):
    return pl.BlockSpec((tm, tn), lambda j, i: (i, j))


def _mm_rope(xg, ssq, w, layer, cos, sin, *, n_q, tm, tn, name):
    m, k = xg.shape
    n_qk = 2 * n_q
    half = cos.shape[1]
    kern = functools.partial(_mm_rope_kernel, n_q_blocks=n_q // tn, k_scale=(2 * half) ** -0.5)
    return pl.pallas_call(
        kern,
        out_shape=jax.ShapeDtypeStruct((m, n_qk), BF16),
        grid=(n_qk // tn, m // tm),
        in_specs=[_x_spec(tm, k), _ssq_spec(ssq, tm), _w_spec(k, tn, layer),
                  pl.BlockSpec((tm, half), lambda j, i: (i, 0)),
                  pl.BlockSpec((tm, half), lambda j, i: (i, 0))],
        out_specs=_tile_spec(tm, tn),
        scratch_shapes=[pltpu.VMEM((k, tn), BF16)],
        compiler_params=_params(2),
        name=name,
    )(xg, ssq, w, cos, sin)


def _mm_plain(xg, ssq, w, layer, *, n, col_offset, tm, tn, out_dtype, relu2, name):
    m, k = xg.shape
    return pl.pallas_call(
        functools.partial(_mm_plain_kernel, relu2=relu2),
        out_shape=jax.ShapeDtypeStruct((m, n), out_dtype),
        grid=(n // tn, m // tm),
        in_specs=[_x_spec(tm, k), _ssq_spec(ssq, tm), _w_spec(k, tn, layer, col_offset // tn)],
        out_specs=_tile_spec(tm, tn),
        scratch_shapes=[pltpu.VMEM((k, tn), BF16)],
        compiler_params=_params(2),
        name=name,
    )(xg, ssq, w)


def _mm_glu(xg, ssq, w, b3, layer, *, tm, tn, name):
    m, k = xg.shape
    n = w.shape[2] // 2
    gate = n // tn
    return pl.pallas_call(
        _mm_glu_kernel,
        out_shape=jax.ShapeDtypeStruct((m, n), F32),
        grid=(n // tn, m // tm),
        in_specs=[_x_spec(tm, k), _ssq_spec(ssq, tm),
                  _w_spec(k, tn, layer), _w_spec(k, tn, layer, gate),
                  pl.BlockSpec((None, 1, tn), lambda j, i: (layer, 0, j)),
                  pl.BlockSpec((None, 1, tn), lambda j, i: (layer, 0, j + gate))],
        out_specs=_tile_spec(tm, tn),
        scratch_shapes=[pltpu.VMEM((k, tn), BF16), pltpu.VMEM((k, tn), BF16)],
        compiler_params=_params(2),
        name=name,
    )(xg, ssq, w, w, b3, b3)


def _mm_residual(a, w, layer, res, *, tm, tn, name, bias3=None, next_gain=None):
    m, k = a.shape
    n = w.shape[2]
    in_specs = [_x_spec(tm, k), pl.BlockSpec(memory_space=pl.ANY)]
    args = [a, w]
    if bias3 is not None:
        in_specs.append(pl.BlockSpec((None, 1, tn), lambda j, i: (layer, 0, j)))
        args.append(bias3)
    in_specs.append(_tile_spec(tm, tn))
    args.append(res)
    out_shape = [jax.ShapeDtypeStruct((m, n), F32)]
    out_specs = [_tile_spec(tm, tn)]
    if next_gain is not None:
        g3, norm_layer = next_gain
        in_specs.append(pl.BlockSpec((None, 1, tn), lambda j, i: (norm_layer, 0, j)))
        args.append(g3)
        out_shape += [jax.ShapeDtypeStruct((m, n), BF16),
                      jax.ShapeDtypeStruct((n // tn, m, LANES), F32)]
        out_specs += [_tile_spec(tm, tn), pl.BlockSpec((None, tm, LANES), lambda j, i: (j, i, 0))]
    out = pl.pallas_call(
        functools.partial(_mm_residual_kernel, layer=layer, has_bias=bias3 is not None,
                          emit=next_gain is not None),
        out_shape=out_shape,
        grid=(n // tn, m // tm),
        in_specs=in_specs,
        out_specs=out_specs,
        scratch_shapes=[pltpu.VMEM((k, tn), BF16), pltpu.VMEM((k, tn), F32),
                        pltpu.SemaphoreType.DMA(())],
        compiler_params=_params(2),
        name=name,
    )(*args)
    return out if next_gain is not None else out[0]


def _group_norm_gate(o, g, gn):
    mu = jnp.mean(o, axis=-1, keepdims=True)
    var = jnp.mean(jnp.square(o - mu), axis=-1, keepdims=True)
    on = (o - mu) * lax.rsqrt(var + EPS) * gn
    return jax.nn.silu(g) * on


def _ret_sample_step(c_ref, q_ref, k_ref, v_ref, g_ref, gn, st_ref, o_ref, so_ref, blk):
    h = pl.program_id(1)
    dec_q = c_ref[h, 0]
    dec_c = c_ref[h, 1]
    intra = c_ref[h, 2]
    dec_k = c_ref[h, 3]
    n_seq = o_ref.shape[0]
    dk = q_ref.shape[1]
    dv = v_ref.shape[1]
    rows = pl.ds(pl.multiple_of(blk * n_seq, n_seq), n_seq)
    q_rows = q_ref[rows, :].astype(F32)
    k_rows = k_ref[rows, :].astype(F32)
    v_rows = v_ref[rows, :].astype(F32)
    o_rows = []
    for bl in range(n_seq):
        qrow = q_rows[bl:bl + 1, :]
        krow = k_rows[bl:bl + 1, :]
        vrow = v_rows[bl:bl + 1, :]
        qcol = jnp.broadcast_to(qrow, (LANES, dk)).T
        kcol = jnp.broadcast_to(krow * dec_k, (LANES, dk)).T
        qk = jnp.sum(qrow * krow, axis=-1, keepdims=True) * intra
        o_tiles = []
        for t in range(dv // LANES):
            lanes = slice(t * LANES, (t + 1) * LANES)
            st = st_ref[bl, :, lanes]
            vt = vrow[:, lanes]
            inter = jnp.sum(qcol * st, axis=0, keepdims=True)
            o_tiles.append(qk * vt + inter * dec_q)
            so_ref[bl, :, lanes] = dec_c * st + kcol * vt
        o_rows.append(jnp.concatenate(o_tiles, axis=1))
    o = jnp.concatenate(o_rows, axis=0)
    o_ref[...] = _group_norm_gate(o, g_ref[rows, :], gn).astype(o_ref.dtype)


def _ret_prompt_step(q_ref, k_ref, v_ref, g_ref, intra_ref, dq_ref, dk_ref, dc_ref, gn, o_ref, s_ref,
                     chunks_per_step):
    @pl.when(pl.program_id(2) == 0)
    def _():
        s_ref[...] = jnp.zeros_like(s_ref)

    c = intra_ref.shape[0]
    s = s_ref[...]
    for cc in range(chunks_per_step):
        rows = slice(cc * c, (cc + 1) * c)
        q = q_ref[rows, :]
        k = k_ref[rows, :]
        v = v_ref[rows, :]
        sc = lax.dot_general(q, k, (((1,), (1,)), ((), ())), preferred_element_type=F32)
        sc = (sc * intra_ref[...]).astype(BF16)
        o = (jnp.dot(sc, v, preferred_element_type=F32)
             + jnp.dot(q, s.astype(BF16), preferred_element_type=F32) * dq_ref[...])
        kd_t = (k.astype(F32) * dk_ref[...]).T.astype(BF16)
        s = dc_ref[...] * s + jnp.dot(kd_t, v, preferred_element_type=F32)
        o_ref[rows, :] = _group_norm_gate(o, g_ref[rows, :], gn).astype(o_ref.dtype)
    s_ref[...] = s


def _retention_kernel(c_ref, q_ref, k_ref, v_ref, g_ref, intra_ref, dq_ref, dk_ref, dc_ref, gn_ref,
                      qs_ref, ks_ref, vs_ref, gs_ref, st_ref, *rest, chunks_per_step):
    o_ref, s_ref, os_ref, so_ref = rest[-4:]
    gn = gn_ref[...]
    blk = pl.program_id(0) * pl.num_programs(2) + pl.program_id(2)
    _ret_sample_step(c_ref, qs_ref, ks_ref, vs_ref, gs_ref, gn, st_ref, os_ref, so_ref, blk)
    _ret_prompt_step(q_ref, k_ref, v_ref, g_ref, intra_ref, dq_ref, dk_ref, dc_ref, gn, o_ref, s_ref,
                     chunks_per_step)


def _retention(consts, qk, v, g, tabs, gn4, state_ret, rs_p_prev, rs_s_prev, layer, *,
               batch, seq, chunks_per_step, name):
    m = qk.shape[0]
    heads = RET_HEADS
    n_ret, n_dec, _, dk, dv = state_ret.shape
    c = RET_CHUNK
    rows = c * chunks_per_step
    steps = seq // rows
    n_seq = n_dec // (batch * steps)
    sample_blk = (batch * seq) // n_dec
    assert n_seq * batch * steps == n_dec and n_seq % 16 == 0

    def tok(b, h, t):
        return (b * steps + t, h)

    def sample_tok(b, h, t):
        return (sample_blk, h)

    state_spec = pl.BlockSpec((None, n_seq, None, dk, dv), lambda b, h, t: (layer, b * steps + t, h, 0, 0))
    in_specs = [
        pl.BlockSpec(memory_space=pltpu.SMEM),
        pl.BlockSpec((rows, dk), tok),
        pl.BlockSpec((rows, dk), lambda b, h, t: (b * steps + t, heads + h)),
        pl.BlockSpec((rows, dv), tok),
        pl.BlockSpec((rows, dv), tok),
        pl.BlockSpec((None, c, c), lambda b, h, t: (h, 0, 0)),
        pl.BlockSpec((None, c, dv), lambda b, h, t: (h, 0, 0)),
        pl.BlockSpec((None, c, dk), lambda b, h, t: (h, 0, 0)),
        pl.BlockSpec((None, 1, dv), lambda b, h, t: (h, 0, 0)),
        pl.BlockSpec((None, None, 1, dv), lambda b, h, t: (layer, h, 0, 0)),
        pl.BlockSpec((n_dec, dk), sample_tok),
        pl.BlockSpec((n_dec, dk), lambda b, h, t: (sample_blk, heads + h)),
        pl.BlockSpec((n_dec, dv), sample_tok),
        pl.BlockSpec((n_dec, dv), sample_tok),
        state_spec,
    ]
    args = [consts, qk, qk, v, g, tabs["intra"], tabs["dq"], tabs["dk"], tabs["dc"], gn4,
            qk, qk, v, g, state_ret]
    aliases = {}
    if rs_p_prev is not None:
        in_specs += [pl.BlockSpec(memory_space=pl.ANY), pl.BlockSpec(memory_space=pl.ANY)]
        aliases = {len(args): 1, len(args) + 1: 3}
        args += [rs_p_prev, rs_s_prev]
    return pl.pallas_call(
        functools.partial(_retention_kernel, chunks_per_step=chunks_per_step),
        out_shape=[jax.ShapeDtypeStruct((m, heads * dv), BF16),
                   jax.ShapeDtypeStruct((n_ret, batch, heads, dk, dv), F32),
                   jax.ShapeDtypeStruct((n_dec, heads * dv), BF16),
                   jax.ShapeDtypeStruct(state_ret.shape, state_ret.dtype)],
        grid=(batch, heads, steps),
        in_specs=in_specs,
        out_specs=[pl.BlockSpec((rows, dv), tok),
                   pl.BlockSpec((None, None, None, dk, dv), lambda b, h, t: (layer, b, h, 0, 0)),
                   pl.BlockSpec((n_seq, dv), tok),
                   state_spec],
        input_output_aliases=aliases,
        compiler_params=_params(3),
        name=name,
    )(*args)


def _place_rows_kernel(src_ref, dst_ref, o_ref):
    del dst_ref
    o_ref[...] = src_ref[...]


def _place_rows(src, dst, first_row, *, name):
    n, w = src.shape
    return pl.pallas_call(
        _place_rows_kernel,
        out_shape=jax.ShapeDtypeStruct(dst.shape, dst.dtype),
        grid=(1,),
        in_specs=[pl.BlockSpec((n, w), lambda i: (0, 0)), pl.BlockSpec(memory_space=pl.ANY)],
        out_specs=pl.BlockSpec((n, w), lambda i: (first_row // n, 0)),
        input_output_aliases={1: 0},
        compiler_params=_params(1),
        name=name,
    )(src, dst)


def _layer_norm_swish(y, g, b):
    mu = jnp.mean(y, axis=-1, keepdims=True)
    var = jnp.mean(jnp.square(y - mu), axis=-1, keepdims=True)
    z = (y - mu) * lax.rsqrt(var + EPS) * g + b
    return jax.nn.silu(z)


CONV_HALO = 32
CONV_ROW_CHUNK = 32
CONV_LANE_CHUNK = 512


def _conv_prompt_kernel(u_ref, halo_ref, w_ref, bdw_ref, lng_ref, lnb_ref, z_ref, tail_ref,
                        win, shifted, y):
    tt, d = u_ref.shape
    width = w_ref.shape[0]
    hist = width - 1
    lead = CONV_HALO - hist
    n_lane_chunks = d // CONV_LANE_CHUNK
    shift_rows = shifted.shape[1]
    first = pl.program_id(1) == 0
    for lc in range(n_lane_chunks):
        lanes = slice(lc * CONV_LANE_CHUNK, (lc + 1) * CONV_LANE_CHUNK)
        win[lc, 0:CONV_HALO, :] = jnp.where(first, 0.0, halo_ref[:, lanes])
        win[lc, CONV_HALO:, :] = u_ref[:, lanes]

    def per_lane_chunk(lc, carry):
        lane0 = pl.multiple_of(lc * CONV_LANE_CHUNK, CONV_LANE_CHUNK)
        for s in range(1, SUBLANES):
            shifted[s - 1, :, :] = win[lc, s:s + shift_rows, :]
        groups = CONV_ROW_CHUNK // SUBLANES
        for rc in range(tt // CONV_ROW_CHUNK):
            r0 = rc * CONV_ROW_CHUNK
            accs = [None] * groups
            for w in range(width):
                s = (lead + w) % SUBLANES
                base = r0 + lead + w - s
                wv = w_ref[w, :, pl.ds(lane0, CONV_LANE_CHUNK)]
                for rg in range(groups):
                    rows = slice(base + rg * SUBLANES, base + (rg + 1) * SUBLANES)
                    src = win[lc, rows, :] if s == 0 else shifted[s - 1, rows, :]
                    term = src * wv
                    accs[rg] = term if accs[rg] is None else accs[rg] + term
            for rg in range(groups):
                y[lc, r0 + rg * SUBLANES:r0 + (rg + 1) * SUBLANES, :] = accs[rg]
        return carry

    lax.fori_loop(0, n_lane_chunks, per_lane_chunk, 0)

    yy = jnp.concatenate([y[lc] for lc in range(n_lane_chunks)], axis=-1) + bdw_ref[...]
    z_ref[...] = _layer_norm_swish(yy, lng_ref[...], lnb_ref[...]).astype(z_ref.dtype)

    @pl.when(pl.program_id(1) == pl.num_programs(1) - 1)
    def _():
        tail_ref[...] = u_ref[tt - hist:tt, :]


def _conv_prompt(u, w_rep, bdw3, lng3, lnb3, layer, *, batch, seq, tt, name):
    m, d = u.shape
    width = w_rep.shape[1]
    hist = width - 1
    steps = seq // tt
    halo_per_tile = tt // CONV_HALO
    n_lane_chunks = d // CONV_LANE_CHUNK
    vec = pl.BlockSpec((None, 1, d), lambda b, i: (layer, 0, 0))
    return pl.pallas_call(
        _conv_prompt_kernel,
        out_shape=[jax.ShapeDtypeStruct((m, d), BF16),
                   jax.ShapeDtypeStruct((batch, hist, d), F32)],
        grid=(batch, steps),
        in_specs=[pl.BlockSpec((tt, d), lambda b, i: (b * steps + i, 0)),
                  pl.BlockSpec((CONV_HALO, d),
                               lambda b, i: (jnp.maximum((b * steps + i) * halo_per_tile - 1, 0), 0)),
                  pl.BlockSpec((None, width, SUBLANES, d), lambda b, i: (layer, 0, 0, 0)),
                  vec, vec, vec],
        out_specs=[pl.BlockSpec((tt, d), lambda b, i: (b * steps + i, 0)),
                   pl.BlockSpec((None, hist, d), lambda b, i: (b, 0, 0))],
        scratch_shapes=[pltpu.VMEM((n_lane_chunks, tt + CONV_HALO, CONV_LANE_CHUNK), F32),
                        pltpu.VMEM((SUBLANES - 1, tt + CONV_HALO - SUBLANES, CONV_LANE_CHUNK), F32),
                        pltpu.VMEM((n_lane_chunks, tt, CONV_LANE_CHUNK), F32)],
        compiler_params=_params(2),
        name=name,
    )(u, u, w_rep, bdw3, lng3, lnb3)


def _conv_sample_kernel(*refs, aliased_inputs):
    buf_ref, u_ref, w_ref, bdw_ref, lng_ref, lnb_ref = refs[:6]
    z_ref, so_ref = refs[6 + aliased_inputs:]
    hist = buf_ref.shape[0]
    u = u_ref[...]
    acc = u * w_ref[hist:hist + 1, :]
    for w in range(hist):
        acc = acc + buf_ref[w] * w_ref[w:w + 1, :]
    y = acc + bdw_ref[...]
    z_ref[...] = _layer_norm_swish(y, lng_ref[...], lnb_ref[...]).astype(z_ref.dtype)
    for w in range(hist - 1):
        so_ref[w] = buf_ref[w + 1]
    so_ref[hist - 1] = u


def _conv_sample(state_t, u, w_dw, bdw3, lng3, lnb3, z, cs_prev, layer, *, first_row, nb, name):
    hist, n_b, d = state_t.shape[1:]
    width = w_dw.shape[1]
    blk0 = first_row // nb
    vec = pl.BlockSpec((None, 1, d), lambda bb: (layer, 0, 0))
    state_spec = pl.BlockSpec((None, hist, nb, d), lambda bb: (layer, 0, bb, 0))
    in_specs = [state_spec,
                pl.BlockSpec((nb, d), lambda bb: (blk0 + bb, 0)),
                pl.BlockSpec((None, width, d), lambda bb: (layer, 0, 0)),
                vec, vec, vec,
                pl.BlockSpec(memory_space=pl.ANY)]
    args = [state_t, u, w_dw, bdw3, lng3, lnb3, z]
    aliases = {6: 0}
    if cs_prev is not None:
        in_specs.append(pl.BlockSpec(memory_space=pl.ANY))
        args.append(cs_prev)
        aliases[7] = 1
    return pl.pallas_call(
        functools.partial(_conv_sample_kernel, aliased_inputs=len(aliases)),
        out_shape=[jax.ShapeDtypeStruct(z.shape, z.dtype),
                   jax.ShapeDtypeStruct(state_t.shape, state_t.dtype)],
        grid=(n_b // nb,),
        in_specs=in_specs,
        out_specs=[pl.BlockSpec((nb, d), lambda bb: (blk0 + bb, 0)), state_spec],
        input_output_aliases=aliases,
        compiler_params=_params(1),
        name=name,
    )(*args)


def _rope_tables(pos, dk):
    half = dk // 2
    freqs = ROPE_BASE ** (-jnp.arange(half, dtype=F32) / half)
    ang = pos[:, None] * freqs[None, :]
    return jnp.cos(ang), jnp.sin(ang)


def _decay_tables(heads, chunk):
    lg = jnp.log1p(-jnp.exp2(-5.0 - jnp.arange(heads, dtype=F32)))
    idx = jnp.arange(chunk, dtype=F32)
    diff = idx[:, None] - idx[None, :]
    intra = jnp.where(diff[None] >= 0.0,
                      jnp.exp(jnp.maximum(diff, 0.0)[None] * lg[:, None, None]), 0.0)
    dec_q = jnp.exp((idx[:, None] + 1.0) * lg[None, :])
    dec_k = jnp.exp((chunk - 1.0 - idx)[:, None] * lg[None, :])
    dec_c = jnp.exp(chunk * lg)
    return intra, dec_q, dec_k, dec_c


def kernel(x_prompt, x_sample, state_ret, state_conv, norm_mix_g, norm_mlp_g, norm_out_g,
           ret_w_in, ret_gn_g, ret_w_out, conv_w_pw1, conv_b_pw1, conv_w_dw, conv_b_dw,
           conv_ln_g, conv_ln_b, conv_w_pw2, conv_b_pw2, mlp_w1, mlp_w2):
    batch, seq, d = x_prompt.shape
    n_dec = x_sample.shape[0]
    depth = norm_mix_g.shape[0]
    n_ret, _, heads, dk, dv = state_ret.shape
    n_conv = state_conv.shape[0]
    m_p = batch * seq
    m = m_p + n_dec
    ret_qk = heads * dk
    ret_v = heads * dv
    d_ff = mlp_w1.shape[2]
    assert heads == RET_HEADS and x_sample.shape[1] == 1

    tm_d = 1040
    tm_deep = 640
    tm_pw2 = 320
    assert m % tm_d == 0 and m % tm_deep == 0 and m % tm_pw2 == 0 and m_p % n_dec == 0

    cos_p, sin_p = _rope_tables(jnp.arange(seq, dtype=F32), dk)
    cos_s, sin_s = _rope_tables(jnp.arange(1, dtype=F32) + float(PAST_LEN), dk)
    cos = jnp.concatenate([jnp.tile(cos_p, (batch, 1)), jnp.tile(cos_s, (n_dec, 1))], axis=0)
    sin = jnp.concatenate([jnp.tile(sin_p, (batch, 1)), jnp.tile(sin_s, (n_dec, 1))], axis=0)

    chunk = math.gcd(seq, RET_CHUNK)
    intra, dec_q, dec_k, dec_c = _decay_tables(heads, chunk)
    tabs = {
        "intra": intra,
        "dq": jnp.broadcast_to(dec_q.T[:, :, None], (heads, chunk, dv)),
        "dk": jnp.broadcast_to(dec_k.T[:, :, None], (heads, chunk, dk)),
        "dc": jnp.broadcast_to(dec_c[:, None, None], (heads, 1, dv)),
    }
    intra1, dec_q1, dec_k1, dec_c1 = _decay_tables(heads, 1)
    consts1 = jnp.stack([dec_q1[0], dec_c1, intra1[:, 0, 0], dec_k1[0]], axis=1)

    mix_g3 = norm_mix_g.reshape(depth, 1, d)
    mlp_g3 = norm_mlp_g.reshape(depth, 1, d)
    out_g3 = norm_out_g.reshape(1, 1, d)
    gn4 = ret_gn_g.reshape(n_ret, heads, 1, dv)
    b_pw1_3 = conv_b_pw1.reshape(n_conv, 1, 2 * d)
    b_dw3 = conv_b_dw.reshape(n_conv, 1, d)
    ln_g3 = conv_ln_g.reshape(n_conv, 1, d)
    ln_b3 = conv_ln_b.reshape(n_conv, 1, d)
    b_pw2_3 = conv_b_pw2.reshape(n_conv, 1, d)
    state_conv_t = jnp.transpose(state_conv, (0, 2, 1, 3))
    w_dw_rep = jnp.broadcast_to(conv_w_dw[:, :, None, :],
                                (n_conv, conv_w_dw.shape[1], SUBLANES, d))

    rs_p = rs_s = cs_t = None
    conv_tails = []
    x, xg, ssq = _assemble(x_prompt.reshape(m_p, d), x_sample.reshape(n_dec, d), mix_g3, 0,
                           name="assemble")
    for i in range(depth):
        j = i // 2
        if i % 2 == 0:
            qk = _mm_rope(xg, ssq, ret_w_in, j, cos, sin, n_q=ret_qk, tm=tm_d, tn=1024,
                          name=f"ret{j}_qk")
            v = _mm_plain(xg, ssq, ret_w_in, j, n=ret_v, col_offset=2 * ret_qk, tm=tm_d, tn=1024,
                          out_dtype=BF16, relu2=False, name=f"ret{j}_v")
            g = _mm_plain(xg, ssq, ret_w_in, j, n=ret_v, col_offset=2 * ret_qk + ret_v, tm=tm_d,
                          tn=1024, out_dtype=F32, relu2=False, name=f"ret{j}_g")
            gated, rs_p, gated_s, rs_s = _retention(consts1, qk, v, g, tabs, gn4, state_ret, rs_p, rs_s,
                                                    j, batch=batch, seq=seq, chunks_per_step=8,
                                                    name=f"ret{j}_mix")
            gated = _place_rows(gated_s, gated, m_p, name=f"ret{j}_place")
            x, xg, ssq = _mm_residual(gated, ret_w_out, j, x, tm=tm_deep, tn=1024,
                                      next_gain=(mlp_g3, i), name=f"ret{j}_out")
        else:
            u = _mm_glu(xg, ssq, conv_w_pw1, b_pw1_3, j, tm=tm_d, tn=512, name=f"conv{j}_pw1")
            z, tail = _conv_prompt(u, w_dw_rep, b_dw3, ln_g3, ln_b3, j, batch=batch, seq=seq,
                                   tt=256, name=f"conv{j}_prompt")
            conv_tails.append(tail)
            z, cs_t = _conv_sample(state_conv_t, u, conv_w_dw, b_dw3, ln_g3, ln_b3, z, cs_t, j,
                                   first_row=m_p, nb=16, name=f"conv{j}_sample")
            x, xg, ssq = _mm_residual(z, conv_w_pw2, j, x, tm=tm_pw2, tn=d, bias3=b_pw2_3,
                                      next_gain=(mlp_g3, i), name=f"conv{j}_pw2")
        a = _mm_plain(xg, ssq, mlp_w1, i, n=d_ff, col_offset=0, tm=tm_d, tn=1024, out_dtype=BF16,
                      relu2=True, name=f"mlp{i}_up")
        if i + 1 < depth:
            x, xg, ssq = _mm_residual(a, mlp_w2, i, x, tm=tm_deep, tn=512,
                                      next_gain=(mix_g3, i + 1), name=f"mlp{i}_down")
        else:
            x = _mm_residual(a, mlp_w2, i, x, tm=tm_deep, tn=512, name=f"mlp{i}_down")

    y_prompt = _rmsnorm(x, out_g3, 0, rows=m_p, row_block=512, first_block=0, out_dtype=F32,
                        name="norm_out_prompt")
    y_sample = _rmsnorm(x, out_g3, 0, rows=n_dec, row_block=n_dec, first_block=m_p // n_dec,
                        out_dtype=F32, name="norm_out_sample")
    return (y_prompt.reshape(batch, seq, d), y_sample.reshape(n_dec, 1, d),
            rs_p, rs_s, jnp.stack(conv_tails), jnp.transpose(cs_t, (0, 2, 1, 3)))
```

```python
import functools
import math

import jax
import jax.numpy as jnp
from jax import lax
from jax.experimental import pallas as pl
from jax.experimental.pallas import tpu as pltpu

F32 = jnp.float32
BF16 = jnp.bfloat16

EPS = 1e-6
ROPE_BASE = 10000.0
PAST_LEN = 16384
RET_CHUNK = 128
RET_HEADS = 8

V7X_VMEM_LIMIT_BYTES = 56 * 1024 * 1024
LANES = 128
SUBLANES = 8

WEIGHT_CAST_ROWS = 256


def _params(n_axes):
    return pltpu.CompilerParams(dimension_semantics=("arbitrary",) * n_axes,
                                vmem_limit_bytes=V7X_VMEM_LIMIT_BYTES)


def _rms_scale(x, g):
    ms = jnp.mean(x * x, axis=-1, keepdims=True)
    return x * lax.rsqrt(ms + EPS) * g


def _rmsnorm_kernel(x_ref, g_ref, o_ref):
    o_ref[...] = _rms_scale(x_ref[...], g_ref[...]).astype(o_ref.dtype)


def _rmsnorm(x, g3, layer, *, rows, row_block, first_block, out_dtype, name):
    d = x.shape[1]
    return pl.pallas_call(
        _rmsnorm_kernel,
        out_shape=jax.ShapeDtypeStruct((rows, d), out_dtype),
        grid=(rows // row_block,),
        in_specs=[pl.BlockSpec((row_block, d), lambda i: (i + first_block, 0)),
                  pl.BlockSpec((None, 1, d), lambda i: (layer, 0, 0))],
        out_specs=pl.BlockSpec((row_block, d), lambda i: (i, 0)),
        compiler_params=_params(1),
        name=name,
    )(x, g3)


def _emit_gained(xn, g_ref, xg_ref, ssq_ref):
    xg_ref[...] = (xn * g_ref[...]).astype(xg_ref.dtype)
    sq = xn * xn
    part = sq[:, 0:LANES]
    for t in range(1, xn.shape[1] // LANES):
        part = part + sq[:, t * LANES:(t + 1) * LANES]
    ssq_ref[...] = part


def _row_rms_inv(ssq_ref, width):
    tot = jnp.sum(jnp.sum(ssq_ref[...], axis=0), axis=-1, keepdims=True)
    return lax.rsqrt(tot / width + EPS)


def _assemble_kernel(xp_ref, xs_ref, g_ref, x_ref, xg_ref, ssq_ref, *, n_prompt_blocks):
    xn = jnp.where(pl.program_id(0) < n_prompt_blocks, xp_ref[...], xs_ref[...])
    x_ref[...] = xn
    _emit_gained(xn, g_ref, xg_ref, ssq_ref)


def _assemble(xp, xs, g3, layer, *, name):
    m_p, d = xp.shape
    rb = xs.shape[0]
    m = m_p + rb
    n_prompt_blocks = m_p // rb
    return pl.pallas_call(
        functools.partial(_assemble_kernel, n_prompt_blocks=n_prompt_blocks),
        out_shape=[jax.ShapeDtypeStruct((m, d), F32), jax.ShapeDtypeStruct((m, d), BF16),
                   jax.ShapeDtypeStruct((1, m, LANES), F32)],
        grid=(m // rb,),
        in_specs=[pl.BlockSpec((rb, d), lambda i: (jnp.minimum(i, n_prompt_blocks - 1), 0)),
                  pl.BlockSpec((rb, d), lambda i: (0, 0)),
                  pl.BlockSpec((None, 1, d), lambda i: (layer, 0, 0))],
        out_specs=[pl.BlockSpec((rb, d), lambda i: (i, 0)), pl.BlockSpec((rb, d), lambda i: (i, 0)),
                   pl.BlockSpec((None, rb, LANES), lambda i: (0, i, 0))],
        compiler_params=_params(1),
        name=name,
    )(xp, xs, g3)


def _cast_weights_once(w_ref, wb_ref):
    @pl.when(pl.program_id(1) == 0)
    def _():
        def body(r, carry):
            rows = pl.ds(pl.multiple_of(r * WEIGHT_CAST_ROWS, WEIGHT_CAST_ROWS), WEIGHT_CAST_ROWS)
            wb_ref[rows, :] = w_ref[rows, :].astype(BF16)
            return carry
        lax.fori_loop(0, w_ref.shape[0] // WEIGHT_CAST_ROWS, body, 0)


def _mm_rope_kernel(x_ref, ssq_ref, w_ref, cos_ref, sin_ref, o_ref, wb_ref, *, n_q_blocks, k_scale):
    _cast_weights_once(w_ref, wb_ref)
    acc = jnp.dot(x_ref[...], wb_ref[...], preferred_element_type=F32)
    acc = acc * _row_rms_inv(ssq_ref, x_ref.shape[1])
    scale = jnp.where(pl.program_id(0) >= n_q_blocks, k_scale, 1.0).astype(F32)
    cos = cos_ref[...]
    sin = sin_ref[...]
    half = cos.shape[1]
    for hh in range(acc.shape[1] // (2 * half)):
        lo = slice(2 * hh * half, (2 * hh + 1) * half)
        hi = slice((2 * hh + 1) * half, (2 * hh + 2) * half)
        x1 = acc[:, lo]
        x2 = acc[:, hi]
        o_ref[:, lo] = ((x1 * cos - x2 * sin) * scale).astype(o_ref.dtype)
        o_ref[:, hi] = ((x1 * sin + x2 * cos) * scale).astype(o_ref.dtype)


def _mm_plain_kernel(x_ref, ssq_ref, w_ref, o_ref, wb_ref, w_stage, sem, *, relu2, layer, col_block):
    _stage_weights(w_ref, layer, w_stage, wb_ref, sem, col_block)
    acc = jnp.dot(x_ref[...], wb_ref[...], preferred_element_type=F32)
    acc = acc * _row_rms_inv(ssq_ref, x_ref.shape[1])
    if relu2:
        acc = jnp.square(jnp.maximum(acc, 0.0))
    o_ref[...] = acc.astype(o_ref.dtype)


def _mm_glu_kernel(x_ref, ssq_ref, wa_ref, wg_ref, ba_ref, bg_ref, o_ref, wab_ref, wgb_ref):
    _cast_weights_once(wa_ref, wab_ref)
    _cast_weights_once(wg_ref, wgb_ref)
    x = x_ref[...]
    r = _row_rms_inv(ssq_ref, x_ref.shape[1])
    a = jnp.dot(x, wab_ref[...], preferred_element_type=F32) * r + ba_ref[...]
    g = jnp.dot(x, wgb_ref[...], preferred_element_type=F32) * r + bg_ref[...]
    o_ref[...] = a * jax.nn.sigmoid(g)


def _stage_weights(w_hbm, layer, w_stage, wb_ref, sem, col_block=0):
    j = pl.program_id(0)
    tn = w_stage.shape[1]

    def tile_copy(jj):
        cols = pl.ds(pl.multiple_of((jj + col_block) * tn, tn), tn)
        return pltpu.make_async_copy(w_hbm.at[layer, :, cols], w_stage, sem)

    @pl.when(pl.program_id(1) == 0)
    def _():
        @pl.when(j == 0)
        def _():
            tile_copy(0).start()

        tile_copy(j).wait()

        def body(r, carry):
            rows = pl.ds(pl.multiple_of(r * WEIGHT_CAST_ROWS, WEIGHT_CAST_ROWS), WEIGHT_CAST_ROWS)
            wb_ref[rows, :] = w_stage[rows, :].astype(BF16)
            return carry
        lax.fori_loop(0, w_stage.shape[0] // WEIGHT_CAST_ROWS, body, 0)

        @pl.when(j + 1 < pl.num_programs(0))
        def _():
            tile_copy(j + 1).start()


def _mm_residual_kernel(*refs, layer, has_bias, emit):
    refs = list(refs)
    x_ref, w_hbm = refs[:2]
    del refs[:2]
    b_ref = refs.pop(0) if has_bias else None
    r_ref = refs.pop(0)
    g_ref = refs.pop(0) if emit else None
    o_ref = refs.pop(0)
    wb_ref, w_stage, sem = refs[-3:]
    del refs[-3:]
    _stage_weights(w_hbm, layer, w_stage, wb_ref, sem)
    acc = jnp.dot(x_ref[...], wb_ref[...], preferred_element_type=F32)
    if has_bias:
        acc = acc + b_ref[...]
    xn = r_ref[...] + acc
    o_ref[...] = xn
    if emit:
        xg_ref, ssq_ref = refs
        _emit_gained(xn, g_ref, xg_ref, ssq_ref)


def _x_spec(tm, k):
    return pl.BlockSpec((tm, k), lambda j, i: (i, 0))


def _ssq_spec(ssq, tm):
    return pl.BlockSpec((ssq.shape[0], tm, LANES), lambda j, i: (0, i, 0))


def _w_spec(k, tn, layer, col_block_offset=0):
    return pl.BlockSpec((None, k, tn), lambda j, i: (layer, 0, j + col_block_offset))


def _tile_spec(tm, tn):
    return pl.BlockSpec((tm, tn), lambda j, i: (i, j))


def _mm_rope(xg, ssq, w, layer, cos, sin, *, n_q, tm, tn, name):
    m, k = xg.shape
    n_qk = 2 * n_q
    half = cos.shape[1]
    kern = functools.partial(_mm_rope_kernel, n_q_blocks=n_q // tn, k_scale=(2 * half) ** -0.5)
    return pl.pallas_call(
        kern,
        out_shape=jax.ShapeDtypeStruct((m, n_qk), BF16),
        grid=(n_qk // tn, m // tm),
        in_specs=[_x_spec(tm, k), _ssq_spec(ssq, tm), _w_spec(k, tn, layer),
                  pl.BlockSpec((tm, half), lambda j, i: (i, 0)),
                  pl.BlockSpec((tm, half), lambda j, i: (i, 0))],
        out_specs=_tile_spec(tm, tn),
        scratch_shapes=[pltpu.VMEM((k, tn), BF16)],
        compiler_params=_params(2),
        name=name,
    )(xg, ssq, w, cos, sin)


def _mm_plain(xg, ssq, w, layer, *, n, col_offset, tm, tn, out_dtype, relu2, name):
    m, k = xg.shape
    return pl.pallas_call(
        functools.partial(_mm_plain_kernel, relu2=relu2, layer=layer, col_block=col_offset // tn),
        out_shape=jax.ShapeDtypeStruct((m, n), out_dtype),
        grid=(n // tn, m // tm),
        in_specs=[_x_spec(tm, k), _ssq_spec(ssq, tm), pl.BlockSpec(memory_space=pl.ANY)],
        out_specs=_tile_spec(tm, tn),
        scratch_shapes=[pltpu.VMEM((k, tn), BF16), pltpu.VMEM((k, tn), F32),
                        pltpu.SemaphoreType.DMA(())],
        compiler_params=_params(2),
        name=name,
    )(xg, ssq, w)


def _mm_glu(xg, ssq, w, b3, layer, *, tm, tn, name):
    m, k = xg.shape
    n = w.shape[2] // 2
    gate = n // tn
    return pl.pallas_call(
        _mm_glu_kernel,
        out_shape=jax.ShapeDtypeStruct((m, n), F32),
        grid=(n // tn, m // tm),
        in_specs=[_x_spec(tm, k), _ssq_spec(ssq, tm),
                  _w_spec(k, tn, layer), _w_spec(k, tn, layer, gate),
                  pl.BlockSpec((None, 1, tn), lambda j, i: (layer, 0, j)),
                  pl.BlockSpec((None, 1, tn), lambda j, i: (layer, 0, j + gate))],
        out_specs=_tile_spec(tm, tn),
        scratch_shapes=[pltpu.VMEM((k, tn), BF16), pltpu.VMEM((k, tn), BF16)],
        compiler_params=_params(2),
        name=name,
    )(xg, ssq, w, w, b3, b3)


def _mm_residual(a, w, layer, res, *, tm, tn, name, bias3=None, next_gain=None):
    m, k = a.shape
    n = w.shape[2]
    in_specs = [_x_spec(tm, k), pl.BlockSpec(memory_space=pl.ANY)]
    args = [a, w]
    if bias3 is not None:
        in_specs.append(pl.BlockSpec((None, 1, tn), lambda j, i: (layer, 0, j)))
        args.append(bias3)
    in_specs.append(_tile_spec(tm, tn))
    args.append(res)
    out_shape = [jax.ShapeDtypeStruct((m, n), F32)]
    out_specs = [_tile_spec(tm, tn)]
    if next_gain is not None:
        g3, norm_layer = next_gain
        in_specs.append(pl.BlockSpec((None, 1, tn), lambda j, i: (norm_layer, 0, j)))
        args.append(g3)
        out_shape += [jax.ShapeDtypeStruct((m, n), BF16),
                      jax.ShapeDtypeStruct((n // tn, m, LANES), F32)]
        out_specs += [_tile_spec(tm, tn), pl.BlockSpec((None, tm, LANES), lambda j, i: (j, i, 0))]
    out = pl.pallas_call(
        functools.partial(_mm_residual_kernel, layer=layer, has_bias=bias3 is not None,
                          emit=next_gain is not None),
        out_shape=out_shape,
        grid=(n // tn, m // tm),
        in_specs=in_specs,
        out_specs=out_specs,
        scratch_shapes=[pltpu.VMEM((k, tn), BF16), pltpu.VMEM((k, tn), F32),
                        pltpu.SemaphoreType.DMA(())],
        compiler_params=_params(2),
        name=name,
    )(*args)
    return out if next_gain is not None else out[0]


def _group_norm_gate(o, g, gn):
    mu = jnp.mean(o, axis=-1, keepdims=True)
    var = jnp.mean(jnp.square(o - mu), axis=-1, keepdims=True)
    on = (o - mu) * lax.rsqrt(var + EPS) * gn
    return jax.nn.silu(g) * on


def _ret_sample_step(c_ref, q_ref, k_ref, v_ref, g_ref, gn, st_ref, o_ref, so_ref, blk):
    h = pl.program_id(1)
    dec_q = c_ref[h, 0]
    dec_c = c_ref[h, 1]
    intra = c_ref[h, 2]
    dec_k = c_ref[h, 3]
    n_seq = o_ref.shape[0]
    dk = q_ref.shape[1]
    dv = v_ref.shape[1]
    rows = pl.ds(pl.multiple_of(blk * n_seq, n_seq), n_seq)
    q_rows = q_ref[rows, :].astype(F32)
    k_rows = k_ref[rows, :].astype(F32)
    v_rows = v_ref[rows, :].astype(F32)
    o_rows = []
    for bl in range(n_seq):
        qrow = q_rows[bl:bl + 1, :]
        krow = k_rows[bl:bl + 1, :]
        vrow = v_rows[bl:bl + 1, :]
        qcol = jnp.broadcast_to(qrow, (LANES, dk)).T
        kcol = jnp.broadcast_to(krow * dec_k, (LANES, dk)).T
        qk = jnp.sum(qrow * krow, axis=-1, keepdims=True) * intra
        o_tiles = []
        for t in range(dv // LANES):
            lanes = slice(t * LANES, (t + 1) * LANES)
            st = st_ref[bl, :, lanes]
            vt = vrow[:, lanes]
            inter = jnp.sum(qcol * st, axis=0, keepdims=True)
            o_tiles.append(qk * vt + inter * dec_q)
            so_ref[bl, :, lanes] = dec_c * st + kcol * vt
        o_rows.append(jnp.concatenate(o_tiles, axis=1))
    o = jnp.concatenate(o_rows, axis=0)
    o_ref[...] = _group_norm_gate(o, g_ref[rows, :], gn).astype(o_ref.dtype)


def _ret_prompt_step(q_ref, k_ref, v_ref, g_ref, intra_ref, dq_ref, dk_ref, dc_ref, gn, o_ref, s_ref,
                     chunks_per_step):
    @pl.when(pl.program_id(2) == 0)
    def _():
        s_ref[...] = jnp.zeros_like(s_ref)

    c = intra_ref.shape[0]
    s = s_ref[...]
    for cc in range(chunks_per_step):
        rows = slice(cc * c, (cc + 1) * c)
        q = q_ref[rows, :]
        k = k_ref[rows, :]
        v = v_ref[rows, :]
        sc = lax.dot_general(q, k, (((1,), (1,)), ((), ())), preferred_element_type=F32)
        sc = (sc * intra_ref[...]).astype(BF16)
        o = (jnp.dot(sc, v, preferred_element_type=F32)
             + jnp.dot(q, s.astype(BF16), preferred_element_type=F32) * dq_ref[...])
        kd_t = (k.astype(F32) * dk_ref[...]).T.astype(BF16)
        s = dc_ref[...] * s + jnp.dot(kd_t, v, preferred_element_type=F32)
        o_ref[rows, :] = _group_norm_gate(o, g_ref[rows, :], gn).astype(o_ref.dtype)
    s_ref[...] = s


def _retention_kernel(c_ref, q_ref, k_ref, v_ref, g_ref, intra_ref, dq_ref, dk_ref, dc_ref, gn_ref,
                      qs_ref, ks_ref, vs_ref, gs_ref, st_ref, *rest, chunks_per_step):
    o_ref, s_ref, os_ref, so_ref = rest[-4:]
    gn = gn_ref[...]
    blk = pl.program_id(0) * pl.num_programs(2) + pl.program_id(2)
    _ret_sample_step(c_ref, qs_ref, ks_ref, vs_ref, gs_ref, gn, st_ref, os_ref, so_ref, blk)
    _ret_prompt_step(q_ref, k_ref, v_ref, g_ref, intra_ref, dq_ref, dk_ref, dc_ref, gn, o_ref, s_ref,
                     chunks_per_step)


def _retention(consts, qk, v, g, tabs, gn4, state_ret, rs_p_prev, rs_s_prev, layer, *,
               batch, seq, chunks_per_step, name):
    m = qk.shape[0]
    heads = RET_HEADS
    n_ret, n_dec, _, dk, dv = state_ret.shape
    c = RET_CHUNK
    rows = c * chunks_per_step
    steps = seq // rows
    n_seq = n_dec // (batch * steps)
    sample_blk = (batch * seq) // n_dec
    assert n_seq * batch * steps == n_dec and n_seq % 16 == 0

    def tok(b, h, t):
        return (b * steps + t, h)

    def sample_tok(b, h, t):
        return (sample_blk, h)

    state_spec = pl.BlockSpec((None, n_seq, None, dk, dv), lambda b, h, t: (layer, b * steps + t, h, 0, 0))
    in_specs = [
        pl.BlockSpec(memory_space=pltpu.SMEM),
        pl.BlockSpec((rows, dk), tok),
        pl.BlockSpec((rows, dk), lambda b, h, t: (b * steps + t, heads + h)),
        pl.BlockSpec((rows, dv), tok),
        pl.BlockSpec((rows, dv), tok),
        pl.BlockSpec((None, c, c), lambda b, h, t: (h, 0, 0)),
        pl.BlockSpec((None, c, dv), lambda b, h, t: (h, 0, 0)),
        pl.BlockSpec((None, c, dk), lambda b, h, t: (h, 0, 0)),
        pl.BlockSpec((None, 1, dv), lambda b, h, t: (h, 0, 0)),
        pl.BlockSpec((None, None, 1, dv), lambda b, h, t: (layer, h, 0, 0)),
        pl.BlockSpec((n_dec, dk), sample_tok),
        pl.BlockSpec((n_dec, dk), lambda b, h, t: (sample_blk, heads + h)),
        pl.BlockSpec((n_dec, dv), sample_tok),
        pl.BlockSpec((n_dec, dv), sample_tok),
        state_spec,
    ]
    args = [consts, qk, qk, v, g, tabs["intra"], tabs["dq"], tabs["dk"], tabs["dc"], gn4,
            qk, qk, v, g, state_ret]
    aliases = {}
    if rs_p_prev is not None:
        in_specs += [pl.BlockSpec(memory_space=pl.ANY), pl.BlockSpec(memory_space=pl.ANY)]
        aliases = {len(args): 1, len(args) + 1: 3}
        args += [rs_p_prev, rs_s_prev]
    return pl.pallas_call(
        functools.partial(_retention_kernel, chunks_per_step=chunks_per_step),
        out_shape=[jax.ShapeDtypeStruct((m, heads * dv), BF16),
                   jax.ShapeDtypeStruct((n_ret, batch, heads, dk, dv), F32),
                   jax.ShapeDtypeStruct((n_dec, heads * dv), BF16),
                   jax.ShapeDtypeStruct(state_ret.shape, state_ret.dtype)],
        grid=(batch, heads, steps),
        in_specs=in_specs,
        out_specs=[pl.BlockSpec((rows, dv), tok),
                   pl.BlockSpec((None, None, None, dk, dv), lambda b, h, t: (layer, b, h, 0, 0)),
                   pl.BlockSpec((n_seq, dv), tok),
                   state_spec],
        input_output_aliases=aliases,
        compiler_params=_params(3),
        name=name,
    )(*args)


def _place_rows_kernel(src_ref, dst_ref, o_ref):
    del dst_ref
    o_ref[...] = src_ref[...]


def _place_rows(src, dst, first_row, *, name):
    n, w = src.shape
    return pl.pallas_call(
        _place_rows_kernel,
        out_shape=jax.ShapeDtypeStruct(dst.shape, dst.dtype),
        grid=(1,),
        in_specs=[pl.BlockSpec((n, w), lambda i: (0, 0)), pl.BlockSpec(memory_space=pl.ANY)],
        out_specs=pl.BlockSpec((n, w), lambda i: (first_row // n, 0)),
        input_output_aliases={1: 0},
        compiler_params=_params(1),
        name=name,
    )(src, dst)


def _layer_norm_swish(y, g, b):
    mu = jnp.mean(y, axis=-1, keepdims=True)
    var = jnp.mean(jnp.square(y - mu), axis=-1, keepdims=True)
    z = (y - mu) * lax.rsqrt(var + EPS) * g + b
    return jax.nn.silu(z)


CONV_HALO = 32
CONV_ROW_CHUNK = 32
CONV_LANE_CHUNK = 512


def _conv_prompt_kernel(u_ref, halo_ref, w_ref, bdw_ref, lng_ref, lnb_ref, z_ref, tail_ref,
                        win, shifted, y):
    tt, d = u_ref.shape
    width = w_ref.shape[0]
    hist = width - 1
    lead = CONV_HALO - hist
    n_lane_chunks = d // CONV_LANE_CHUNK
    shift_rows = shifted.shape[1]
    first = pl.program_id(1) == 0
    for lc in range(n_lane_chunks):
        lanes = slice(lc * CONV_LANE_CHUNK, (lc + 1) * CONV_LANE_CHUNK)
        win[lc, 0:CONV_HALO, :] = jnp.where(first, 0.0, halo_ref[:, lanes])
        win[lc, CONV_HALO:, :] = u_ref[:, lanes]

    def per_lane_chunk(lc, carry):
        lane0 = pl.multiple_of(lc * CONV_LANE_CHUNK, CONV_LANE_CHUNK)
        for s in range(1, SUBLANES):
            shifted[s - 1, :, :] = win[lc, s:s + shift_rows, :]
        groups = CONV_ROW_CHUNK // SUBLANES
        for rc in range(tt // CONV_ROW_CHUNK):
            r0 = rc * CONV_ROW_CHUNK
            accs = [None] * groups
            for w in range(width):
                s = (lead + w) % SUBLANES
                base = r0 + lead + w - s
                wv = w_ref[w, :, pl.ds(lane0, CONV_LANE_CHUNK)]
                for rg in range(groups):
                    rows = slice(base + rg * SUBLANES, base + (rg + 1) * SUBLANES)
                    src = win[lc, rows, :] if s == 0 else shifted[s - 1, rows, :]
                    term = src * wv
                    accs[rg] = term if accs[rg] is None else accs[rg] + term
            for rg in range(groups):
                y[lc, r0 + rg * SUBLANES:r0 + (rg + 1) * SUBLANES, :] = accs[rg]
        return carry

    lax.fori_loop(0, n_lane_chunks, per_lane_chunk, 0)

    yy = jnp.concatenate([y[lc] for lc in range(n_lane_chunks)], axis=-1) + bdw_ref[...]
    z_ref[...] = _layer_norm_swish(yy, lng_ref[...], lnb_ref[...]).astype(z_ref.dtype)

    @pl.when(pl.program_id(1) == pl.num_programs(1) - 1)
    def _():
        tail_ref[...] = u_ref[tt - hist:tt, :]


def _conv_prompt(u, w_rep, bdw3, lng3, lnb3, layer, *, batch, seq, tt, name):
    m, d = u.shape
    width = w_rep.shape[1]
    hist = width - 1
    steps = seq // tt
    halo_per_tile = tt // CONV_HALO
    n_lane_chunks = d // CONV_LANE_CHUNK
    vec = pl.BlockSpec((None, 1, d), lambda b, i: (layer, 0, 0))
    return pl.pallas_call(
        _conv_prompt_kernel,
        out_shape=[jax.ShapeDtypeStruct((m, d), BF16),
                   jax.ShapeDtypeStruct((batch, hist, d), F32)],
        grid=(batch, steps),
        in_specs=[pl.BlockSpec((tt, d), lambda b, i: (b * steps + i, 0)),
                  pl.BlockSpec((CONV_HALO, d),
                               lambda b, i: (jnp.maximum((b * steps + i) * halo_per_tile - 1, 0), 0)),
                  pl.BlockSpec((None, width, SUBLANES, d), lambda b, i: (layer, 0, 0, 0)),
                  vec, vec, vec],
        out_specs=[pl.BlockSpec((tt, d), lambda b, i: (b * steps + i, 0)),
                   pl.BlockSpec((None, hist, d), lambda b, i: (b, 0, 0))],
        scratch_shapes=[pltpu.VMEM((n_lane_chunks, tt + CONV_HALO, CONV_LANE_CHUNK), F32),
                        pltpu.VMEM((SUBLANES - 1, tt + CONV_HALO - SUBLANES, CONV_LANE_CHUNK), F32),
                        pltpu.VMEM((n_lane_chunks, tt, CONV_LANE_CHUNK), F32)],
        compiler_params=_params(2),
        name=name,
    )(u, u, w_rep, bdw3, lng3, lnb3)


def _conv_sample_kernel(*refs, aliased_inputs):
    buf_ref, u_ref, w_ref, bdw_ref, lng_ref, lnb_ref = refs[:6]
    z_ref, so_ref = refs[6 + aliased_inputs:]
    hist = buf_ref.shape[0]
    u = u_ref[...]
    acc = u * w_ref[hist:hist + 1, :]
    for w in range(hist):
        acc = acc + buf_ref[w] * w_ref[w:w + 1, :]
    y = acc + bdw_ref[...]
    z_ref[...] = _layer_norm_swish(y, lng_ref[...], lnb_ref[...]).astype(z_ref.dtype)
    for w in range(hist - 1):
        so_ref[w] = buf_ref[w + 1]
    so_ref[hist - 1] = u


def _conv_sample(state_t, u, w_dw, bdw3, lng3, lnb3, z, cs_prev, layer, *, first_row, nb, name):
    hist, n_b, d = state_t.shape[1:]
    width = w_dw.shape[1]
    blk0 = first_row // nb
    vec = pl.BlockSpec((None, 1, d), lambda bb: (layer, 0, 0))
    state_spec = pl.BlockSpec((None, hist, nb, d), lambda bb: (layer, 0, bb, 0))
    in_specs = [state_spec,
                pl.BlockSpec((nb, d), lambda bb: (blk0 + bb, 0)),
                pl.BlockSpec((None, width, d), lambda bb: (layer, 0, 0)),
                vec, vec, vec,
                pl.BlockSpec(memory_space=pl.ANY)]
    args = [state_t, u, w_dw, bdw3, lng3, lnb3, z]
    aliases = {6: 0}
    if cs_prev is not None:
        in_specs.append(pl.BlockSpec(memory_space=pl.ANY))
        args.append(cs_prev)
        aliases[7] = 1
    return pl.pallas_call(
        functools.partial(_conv_sample_kernel, aliased_inputs=len(aliases)),
        out_shape=[jax.ShapeDtypeStruct(z.shape, z.dtype),
                   jax.ShapeDtypeStruct(state_t.shape, state_t.dtype)],
        grid=(n_b // nb,),
        in_specs=in_specs,
        out_specs=[pl.BlockSpec((nb, d), lambda bb: (blk0 + bb, 0)), state_spec],
        input_output_aliases=aliases,
        compiler_params=_params(1),
        name=name,
    )(*args)


def _rope_tables(pos, dk):
    half = dk // 2
    freqs = ROPE_BASE ** (-jnp.arange(half, dtype=F32) / half)
    ang = pos[:, None] * freqs[None, :]
    return jnp.cos(ang), jnp.sin(ang)


def _decay_tables(heads, chunk):
    lg = jnp.log1p(-jnp.exp2(-5.0 - jnp.arange(heads, dtype=F32)))
    idx = jnp.arange(chunk, dtype=F32)
    diff = idx[:, None] - idx[None, :]
    intra = jnp.where(diff[None] >= 0.0,
                      jnp.exp(jnp.maximum(diff, 0.0)[None] * lg[:, None, None]), 0.0)
    dec_q = jnp.exp((idx[:, None] + 1.0) * lg[None, :])
    dec_k = jnp.exp((chunk - 1.0 - idx)[:, None] * lg[None, :])
    dec_c = jnp.exp(chunk * lg)
    return intra, dec_q, dec_k, dec_c


def kernel(x_prompt, x_sample, state_ret, state_conv, norm_mix_g, norm_mlp_g, norm_out_g,
           ret_w_in, ret_gn_g, ret_w_out, conv_w_pw1, conv_b_pw1, conv_w_dw, conv_b_dw,
           conv_ln_g, conv_ln_b, conv_w_pw2, conv_b_pw2, mlp_w1, mlp_w2):
    batch, seq, d = x_prompt.shape
    n_dec = x_sample.shape[0]
    depth = norm_mix_g.shape[0]
    n_ret, _, heads, dk, dv = state_ret.shape
    n_conv = state_conv.shape[0]
    m_p = batch * seq
    m = m_p + n_dec
    ret_qk = heads * dk
    ret_v = heads * dv
    d_ff = mlp_w1.shape[2]
    assert heads == RET_HEADS and x_sample.shape[1] == 1

    tm_d = 1040
    tm_staged = 1664
    tm_deep = 640
    tm_pw2 = 320
    assert m % tm_d == 0 and m % tm_staged == 0 and m % tm_deep == 0 and m % tm_pw2 == 0
    assert m_p % n_dec == 0

    cos_p, sin_p = _rope_tables(jnp.arange(seq, dtype=F32), dk)
    cos_s, sin_s = _rope_tables(jnp.arange(1, dtype=F32) + float(PAST_LEN), dk)
    cos = jnp.concatenate([jnp.tile(cos_p, (batch, 1)), jnp.tile(cos_s, (n_dec, 1))], axis=0)
    sin = jnp.concatenate([jnp.tile(sin_p, (batch, 1)), jnp.tile(sin_s, (n_dec, 1))], axis=0)

    chunk = math.gcd(seq, RET_CHUNK)
    intra, dec_q, dec_k, dec_c = _decay_tables(heads, chunk)
    tabs = {
        "intra": intra,
        "dq": jnp.broadcast_to(dec_q.T[:, :, None], (heads, chunk, dv)),
        "dk": jnp.broadcast_to(dec_k.T[:, :, None], (heads, chunk, dk)),
        "dc": jnp.broadcast_to(dec_c[:, None, None], (heads, 1, dv)),
    }
    intra1, dec_q1, dec_k1, dec_c1 = _decay_tables(heads, 1)
    consts1 = jnp.stack([dec_q1[0], dec_c1, intra1[:, 0, 0], dec_k1[0]], axis=1)

    mix_g3 = norm_mix_g.reshape(depth, 1, d)
    mlp_g3 = norm_mlp_g.reshape(depth, 1, d)
    out_g3 = norm_out_g.reshape(1, 1, d)
    gn4 = ret_gn_g.reshape(n_ret, heads, 1, dv)
    b_pw1_3 = conv_b_pw1.reshape(n_conv, 1, 2 * d)
    b_dw3 = conv_b_dw.reshape(n_conv, 1, d)
    ln_g3 = conv_ln_g.reshape(n_conv, 1, d)
    ln_b3 = conv_ln_b.reshape(n_conv, 1, d)
    b_pw2_3 = conv_b_pw2.reshape(n_conv, 1, d)
    state_conv_t = jnp.transpose(state_conv, (0, 2, 1, 3))
    w_dw_rep = jnp.broadcast_to(conv_w_dw[:, :, None, :],
                                (n_conv, conv_w_dw.shape[1], SUBLANES, d))

    rs_p = rs_s = cs_t = None
    conv_tails = []
    x, xg, ssq = _assemble(x_prompt.reshape(m_p, d), x_sample.reshape(n_dec, d), mix_g3, 0,
                           name="assemble")
    for i in range(depth):
        j = i // 2
        if i % 2 == 0:
            qk = _mm_rope(xg, ssq, ret_w_in, j, cos, sin, n_q=ret_qk, tm=tm_d, tn=1024,
                          name=f"ret{j}_qk")
            v = _mm_plain(xg, ssq, ret_w_in, j, n=ret_v, col_offset=2 * ret_qk, tm=tm_staged, tn=1024,
                          out_dtype=BF16, relu2=False, name=f"ret{j}_v")
            g = _mm_plain(xg, ssq, ret_w_in, j, n=ret_v, col_offset=2 * ret_qk + ret_v, tm=tm_staged,
                          tn=1024, out_dtype=F32, relu2=False, name=f"ret{j}_g")
            gated, rs_p, gated_s, rs_s = _retention(consts1, qk, v, g, tabs, gn4, state_ret, rs_p, rs_s,
                                                    j, batch=batch, seq=seq, chunks_per_step=8,
                                                    name=f"ret{j}_mix")
            gated = _place_rows(gated_s, gated, m_p, name=f"ret{j}_place")
            x, xg, ssq = _mm_residual(gated, ret_w_out, j, x, tm=tm_deep, tn=1024,
                                      next_gain=(mlp_g3, i), name=f"ret{j}_out")
        else:
            u = _mm_glu(xg, ssq, conv_w_pw1, b_pw1_3, j, tm=tm_d, tn=512, name=f"conv{j}_pw1")
            z, tail = _conv_prompt(u, w_dw_rep, b_dw3, ln_g3, ln_b3, j, batch=batch, seq=seq,
                                   tt=256, name=f"conv{j}_prompt")
            conv_tails.append(tail)
            z, cs_t = _conv_sample(state_conv_t, u, conv_w_dw, b_dw3, ln_g3, ln_b3, z, cs_t, j,
                                   first_row=m_p, nb=16, name=f"conv{j}_sample")
            x, xg, ssq = _mm_residual(z, conv_w_pw2, j, x, tm=tm_pw2, tn=d, bias3=b_pw2_3,
                                      next_gain=(mlp_g3, i), name=f"conv{j}_pw2")
        a = _mm_plain(xg, ssq, mlp_w1, i, n=d_ff, col_offset=0, tm=tm_staged, tn=1024, out_dtype=BF16,
                      relu2=True, name=f"mlp{i}_up")
        if i + 1 < depth:
            x, xg, ssq = _mm_residual(a, mlp_w2, i, x, tm=tm_deep, tn=512,
                                      next_gain=(mix_g3, i + 1), name=f"mlp{i}_down")
        else:
            x = _mm_residual(a, mlp_w2, i, x, tm=tm_deep, tn=512, name=f"mlp{i}_down")

    y_prompt = _rmsnorm(x, out_g3, 0, rows=m_p, row_block=512, first_block=0, out_dtype=F32,
                        name="norm_out_prompt")
    y_sample = _rmsnorm(x, out_g3, 0, rows=n_dec, row_block=n_dec, first_block=m_p // n_dec,
                        out_dtype=F32, name="norm_out_sample")
    return (y_prompt.reshape(batch, seq, d), y_sample.reshape(n_dec, 1, d),
            rs_p, rs_s, jnp.stack(conv_tails), jnp.transpose(cs_t, (0, 2, 1, 3)))
```
